```python
import jax, jax.numpy as jnp
from jax import lax
import numpy as np

D_MODEL = 1024
BATCH = 4
SEQ = 8192
DEPTH = 1

HEAD_DIM = 64
N_HEADS = 8
DILATED_GROUPS = ((128, 1), (512, 4), (2048, 16))
N_GROUPS = len(DILATED_GROUPS)
ATTN_WIDTH = N_HEADS * HEAD_DIM
CONV_CHANNELS = 512
CONV_WIDTH = 31
D_FF = -(-8 * D_MODEL // (3 * 256)) * 256
ROPE_THETA = 10000.0
BLOCK = 128
LN_EPS = 1e-5
QKV_COLS = N_GROUPS * 3 * ATTN_WIDTH
IN_COLS = QKV_COLS + 2 * CONV_CHANNELS + 2 * D_MODEL

kernel_name = "hybrid_dilated_attn_conformer_conv_deepnorm"


def layer_norm(x, g, b):
    xf = x.astype(jnp.float32)
    mu = jnp.mean(xf, axis=-1, keepdims=True)
    var = jnp.mean(jnp.square(xf - mu), axis=-1, keepdims=True)
    y = (xf - mu) * lax.rsqrt(var + LN_EPS)
    return (y * g.astype(jnp.float32) + b.astype(jnp.float32)).astype(x.dtype)


def rotary_tables(seq_len):
    inv_freq = 1.0 / (ROPE_THETA ** (jnp.arange(0, HEAD_DIM, 2, dtype=jnp.float32) / HEAD_DIM))
    ang = jnp.arange(seq_len, dtype=jnp.float32)[:, None] * inv_freq[None, :]
    return jnp.cos(ang)[:, None, :], jnp.sin(ang)[:, None, :]


def apply_rotary(x, cos, sin):
    xf = x.astype(jnp.float32)
    x1, x2 = jnp.split(xf, 2, axis=-1)
    return jnp.concatenate([x1 * cos - x2 * sin, x2 * cos + x1 * sin], axis=-1).astype(x.dtype)


def dilated_window_attention(q, k, v, window, dilation):
    B, S, H, Dh = q.shape
    n_keys = window // dilation
    L = S // dilation
    n_blk = -(-L // BLOCK)
    Lp = n_blk * BLOCK

    def by_residue(a):
        return a.reshape(B, L, dilation, H, Dh).transpose(0, 2, 1, 3, 4)

    qb = jnp.pad(by_residue(q), ((0, 0), (0, 0), (0, Lp - L), (0, 0), (0, 0)))
    qb = qb.reshape(B, dilation, n_blk, BLOCK, H, Dh)
    pad_k = ((0, 0), (0, 0), (BLOCK, Lp - L), (0, 0), (0, 0))
    kp = jnp.pad(by_residue(k), pad_k)
    vp = jnp.pad(by_residue(v), pad_k)

    def band(a):
        prev = a[:, :, :Lp].reshape(B, dilation, n_blk, BLOCK, H, Dh)
        cur = a[:, :, BLOCK:].reshape(B, dilation, n_blk, BLOCK, H, Dh)
        return jnp.concatenate([prev, cur], axis=3)

    kb, vb = band(kp), band(vp)
    s = jnp.einsum('brnqhd,brnkhd->brnhqk', qb, kb,
                   preferred_element_type=jnp.float32) * (Dh ** -0.5)
    iq = jnp.arange(BLOCK)[:, None]
    ik = jnp.arange(2 * BLOCK)[None, :]
    dist = iq + BLOCK - ik
    k_pos = jnp.arange(n_blk)[:, None, None] * BLOCK - BLOCK + ik[None]
    mask = (dist >= 0) & (dist <= n_keys) & (k_pos >= 0)
    s = jnp.where(mask[None, None, :, None], s, -jnp.inf)
    m = jnp.max(s, axis=-1, keepdims=True)
    p = jnp.exp(s - m)
    den = jnp.sum(p, axis=-1, keepdims=True)
    o = jnp.einsum('brnhqk,brnkhd->brnqhd', p, vb.astype(jnp.float32))
    o = o / den.transpose(0, 1, 2, 4, 3, 5)
    lse = (m + jnp.log(den))[..., 0].transpose(0, 1, 2, 4, 3)
    o = o.reshape(B, dilation, Lp, H, Dh)[:, :, :L].transpose(0, 2, 1, 3, 4).reshape(B, S, H, Dh)
    lse = lse.reshape(B, dilation, Lp, H)[:, :, :L].transpose(0, 2, 1, 3).reshape(B, S, H)
    return o, lse


def hybrid_mixer(x, w_in, conv_w, conv_b, conv_ln_g, conv_ln_b, w_attn_out, w_conv_out, w_o):
    B, S, _ = x.shape
    h = x @ w_in
    qkv = h[..., :QKV_COLS].reshape(B, S, N_GROUPS, 3, N_HEADS, HEAD_DIM)
    conv_in = h[..., QKV_COLS:QKV_COLS + 2 * CONV_CHANNELS]
    gates = jax.nn.sigmoid(h[..., QKV_COLS + 2 * CONV_CHANNELS:].astype(jnp.float32)).astype(x.dtype)
    g_attn, g_conv = jnp.split(gates, 2, axis=-1)

    cos, sin = rotary_tables(S)
    outs, lses = [], []
    for g, (window, dilation) in enumerate(DILATED_GROUPS):
        q = apply_rotary(qkv[:, :, g, 0], cos, sin)
        k = apply_rotary(qkv[:, :, g, 1], cos, sin)
        o, lse = dilated_window_attention(q, k, qkv[:, :, g, 2], window, dilation)
        outs.append(o)
        lses.append(lse)
    wts = jax.nn.softmax(jnp.stack(lses, axis=0), axis=0)
    o = jnp.sum(wts[..., None] * jnp.stack(outs, axis=0), axis=0).astype(x.dtype)
    y_attn = o.reshape(B, S, ATTN_WIDTH) @ w_attn_out

    a, b = jnp.split(conv_in, 2, axis=-1)
    u = a * jax.nn.sigmoid(b)
    u = lax.conv_general_dilated(u, conv_w[:, None, :].astype(u.dtype), window_strides=(1,),
                                 padding=[(CONV_WIDTH - 1, 0)],
                                 dimension_numbers=('NWC', 'WIO', 'NWC'),
                                 feature_group_count=CONV_CHANNELS) + conv_b
    u = jax.nn.silu(layer_norm(u, conv_ln_g, conv_ln_b))
    y_conv = u @ w_conv_out

    merged = g_attn * y_attn + g_conv * y_conv
    return merged @ w_o


def swiglu_ffn(x, w_gate, w_up, w_down):
    return (jax.nn.silu(x @ w_gate) * (x @ w_up)) @ w_down


def setup_inputs(seed: int = 0) -> dict:
    key = jax.random.key(seed)
    ks = jax.random.split(key, 17)
    f32 = jnp.float32
    beta = (8 * DEPTH) ** -0.25
    nrm = lambda k, shape, scale: jax.random.normal(k, shape, f32) * scale
    return {
        "x": jax.random.normal(ks[0], (BATCH, SEQ, D_MODEL), f32),
        "w_in": nrm(ks[1], (DEPTH, D_MODEL, IN_COLS), D_MODEL ** -0.5),
        "conv_w": nrm(ks[2], (DEPTH, CONV_WIDTH, CONV_CHANNELS), CONV_WIDTH ** -0.5),
        "conv_b": nrm(ks[3], (DEPTH, CONV_CHANNELS), 0.02),
        "conv_ln_g": 1.0 + nrm(ks[4], (DEPTH, CONV_CHANNELS), 0.05),
        "conv_ln_b": nrm(ks[5], (DEPTH, CONV_CHANNELS), 0.02),
        "w_attn_out": nrm(ks[6], (DEPTH, ATTN_WIDTH, D_MODEL), ATTN_WIDTH ** -0.5),
        "w_conv_out": nrm(ks[7], (DEPTH, CONV_CHANNELS, D_MODEL), CONV_CHANNELS ** -0.5),
        "w_o": nrm(ks[8], (DEPTH, D_MODEL, D_MODEL), beta * D_MODEL ** -0.5),
        "ln1_g": 1.0 + nrm(ks[9], (DEPTH, D_MODEL), 0.05),
        "ln1_b": nrm(ks[10], (DEPTH, D_MODEL), 0.02),
        "w_ffn_gate": nrm(ks[11], (DEPTH, D_MODEL, D_FF), D_MODEL ** -0.5),
        "w_ffn_up": nrm(ks[12], (DEPTH, D_MODEL, D_FF), D_MODEL ** -0.5),
        "w_ffn_down": nrm(ks[13], (DEPTH, D_FF, D_MODEL), beta * D_FF ** -0.5),
        "ln2_g": 1.0 + nrm(ks[14], (DEPTH, D_MODEL), 0.05),
        "ln2_b": nrm(ks[15], (DEPTH, D_MODEL), 0.02),
    }


def reference(x, w_in, conv_w, conv_b, conv_ln_g, conv_ln_b, w_attn_out, w_conv_out, w_o,
              ln1_g, ln1_b, w_ffn_gate, w_ffn_up, w_ffn_down, ln2_g, ln2_b):
    alpha = (2 * DEPTH) ** 0.25
    for l in range(DEPTH):
        mix = hybrid_mixer(x, w_in[l], conv_w[l], conv_b[l], conv_ln_g[l], conv_ln_b[l],
                           w_attn_out[l], w_conv_out[l], w_o[l])
        x = layer_norm(alpha * x + mix, ln1_g[l], ln1_b[l])
        ff = swiglu_ffn(x, w_ffn_gate[l], w_ffn_up[l], w_ffn_down[l])
        x = layer_norm(alpha * x + ff, ln2_g[l], ln2_b[l])
    return x
```

```python
import functools

import jax
import jax.numpy as jnp
from jax import lax
from jax.experimental import pallas as pl
from jax.experimental.pallas import tpu as pltpu

D_MODEL = 1024
HEAD_DIM = 64
N_HEADS = 8
ATTN_WIDTH = N_HEADS * HEAD_DIM
DILATED_GROUPS = ((128, 1), (512, 4), (2048, 16))
N_GROUPS = len(DILATED_GROUPS)
CONV_CHANNELS = 512
CONV_WIDTH = 31
ROPE_THETA = 10000.0
LN_EPS = 1e-5
QKV_COLS = N_GROUPS * 3 * ATTN_WIDTH
CONV_COL0 = QKV_COLS
GATE_COL0 = QKV_COLS + 2 * CONV_CHANNELS

LANES = 128
HEADS_PER_VREG = LANES // HEAD_DIM
N_HEAD_PAIRS = N_HEADS // HEADS_PER_VREG
LSE_LANES_PER_HEAD = LANES // N_HEADS
Q_BLOCK = 128
CONV_HALO = 32
MASK_VALUE = -1e30
VMEM_LIMIT_BYTES = 56 * 1024 * 1024

F32 = jnp.float32
BF16 = jnp.bfloat16


def _layer_norm_rows(h, g, b):
    mu = jnp.mean(h, axis=-1, keepdims=True)
    d = h - mu
    var = jnp.mean(d * d, axis=-1, keepdims=True)
    return d * lax.rsqrt(var + LN_EPS) * g + b


def _in_proj_kernel(x_ref, w_ref, cos_ref, sin_ref, *out_refs):
    qkv_refs = out_refs[:3 * N_GROUPS]
    u_ref, ga_ref, gc_ref = out_refs[3 * N_GROUPS:]
    xb = x_ref[...].astype(BF16)
    tm = xb.shape[0]
    cos_t = cos_ref[...]
    sin_t = sin_ref[...]
    lane = lax.broadcasted_iota(jnp.int32, (tm, LANES), 1)
    first_half = (lane & (HEAD_DIM // 2)) == 0

    def mm(col0, width):
        return jnp.dot(xb, w_ref[:, col0:col0 + width], preferred_element_type=F32)

    def rope(h):
        outs = []
        for c in range(h.shape[1] // LANES):
            hs = h[:, c * LANES:(c + 1) * LANES]
            partner = jnp.where(first_half,
                                pltpu.roll(hs, LANES - HEAD_DIM // 2, 1),
                                pltpu.roll(hs, HEAD_DIM // 2, 1))
            outs.append(hs * cos_t + partner * sin_t)
        return jnp.concatenate(outs, axis=1)

    for g in range(N_GROUPS):
        base = g * 3 * ATTN_WIDTH
        q = rope(mm(base, ATTN_WIDTH)) * (HEAD_DIM ** -0.5)
        qkv_refs[3 * g][...] = q.astype(BF16)
        qkv_refs[3 * g + 1][...] = rope(mm(base + ATTN_WIDTH, ATTN_WIDTH)).astype(BF16)
        qkv_refs[3 * g + 2][...] = mm(base + 2 * ATTN_WIDTH, ATTN_WIDTH).astype(BF16)

    a = mm(CONV_COL0, CONV_CHANNELS)
    b = mm(CONV_COL0 + CONV_CHANNELS, CONV_CHANNELS)
    u_ref[...] = (a * jax.nn.sigmoid(b)).astype(BF16)
    ga_ref[...] = jax.nn.sigmoid(mm(GATE_COL0, D_MODEL)).astype(BF16)
    gc_ref[...] = jax.nn.sigmoid(mm(GATE_COL0 + D_MODEL, D_MODEL)).astype(BF16)


def _in_proj(x2, w_in_b, cos_t, sin_t, seq, tm):
    tokens = x2.shape[0]
    n_seq_tiles = seq // tm
    row = lambda i: (i, 0)
    tab = lambda i: (i % n_seq_tiles, 0)
    out_shape = ([jax.ShapeDtypeStruct((tokens, ATTN_WIDTH), BF16)] * (3 * N_GROUPS)
                 + [jax.ShapeDtypeStruct((tokens, CONV_CHANNELS), BF16)]
                 + [jax.ShapeDtypeStruct((tokens, D_MODEL), BF16)] * 2)
    out_specs = ([pl.BlockSpec((tm, ATTN_WIDTH), row)] * (3 * N_GROUPS)
                 + [pl.BlockSpec((tm, CONV_CHANNELS), row)]
                 + [pl.BlockSpec((tm, D_MODEL), row)] * 2)
    return pl.pallas_call(
        _in_proj_kernel,
        out_shape=out_shape,
        grid=(tokens // tm,),
        in_specs=[
            pl.BlockSpec((tm, D_MODEL), row),
            pl.BlockSpec(w_in_b.shape, lambda i: (0, 0), pipeline_mode=pl.Buffered(1)),
            pl.BlockSpec((tm, LANES), tab),
            pl.BlockSpec((tm, LANES), tab),
        ],
        out_specs=out_specs,
        compiler_params=pltpu.CompilerParams(
            dimension_semantics=("parallel",), vmem_limit_bytes=VMEM_LIMIT_BYTES),
        name="in_proj",
    )(x2, w_in_b, cos_t, sin_t)


def _conv_kernel(u_ref, halo_ref, w_ref, b_ref, g_ref, beta_ref, o_ref, win_ref, *, rows_per_chunk):
    tc = u_ref.shape[1]
    halo = halo_ref[0].astype(F32)
    halo = jnp.where(pl.program_id(1) > 0, halo, 0.0)
    win_ref[0:CONV_HALO, :] = halo
    win_ref[CONV_HALO:CONV_HALO + tc, :] = u_ref[0].astype(F32)
    first_tap_row = CONV_HALO - (CONV_WIDTH - 1)
    bias = b_ref[...]
    gain = g_ref[...]
    beta = beta_ref[...]
    for r0 in range(0, tc, rows_per_chunk):
        acc = jnp.zeros((rows_per_chunk, CONV_CHANNELS), F32)
        for j in range(CONV_WIDTH):
            start = r0 + first_tap_row + j
            acc = acc + win_ref[start:start + rows_per_chunk, :] * w_ref[j:j + 1, :]
        y = _layer_norm_rows(acc + bias, gain, beta)
        o_ref[0, r0:r0 + rows_per_chunk, :] = (y * jax.nn.sigmoid(y)).astype(BF16)


def _conv_branch(u3, conv_w, conv_b, ln_g, ln_b, tc):
    batch, seq, _ = u3.shape
    halo_blocks_per_tile = tc // CONV_HALO
    vec = pl.BlockSpec((1, CONV_CHANNELS), lambda b, i: (0, 0))
    return pl.pallas_call(
        functools.partial(_conv_kernel, rows_per_chunk=64),
        out_shape=jax.ShapeDtypeStruct(u3.shape, BF16),
        grid=(batch, seq // tc),
        in_specs=[
            pl.BlockSpec((1, tc, CONV_CHANNELS), lambda b, i: (b, i, 0)),
            pl.BlockSpec((1, CONV_HALO, CONV_CHANNELS),
                         lambda b, i: (b, jnp.maximum(i * halo_blocks_per_tile - 1, 0), 0)),
            pl.BlockSpec((CONV_WIDTH, CONV_CHANNELS), lambda b, i: (0, 0)),
            vec, vec, vec,
        ],
        out_specs=pl.BlockSpec((1, tc, CONV_CHANNELS), lambda b, i: (b, i, 0)),
        scratch_shapes=[pltpu.VMEM((CONV_HALO + tc, CONV_CHANNELS), F32)],
        compiler_params=pltpu.CompilerParams(
            dimension_semantics=("parallel", "parallel"), vmem_limit_bytes=VMEM_LIMIT_BYTES),
        name="conv_branch",
    )(u3, u3, conv_w, conv_b, ln_g, ln_b)


def _attn_kernel(q_ref, kc_ref, kp_ref, vc_ref, vp_ref, o_ref, lse_ref, kbuf, vbuf, bias_ref):
    tq = q_ref.shape[1]
    tile = pl.program_id(2)
    kbuf[0:Q_BLOCK, :] = kp_ref[0]
    kbuf[Q_BLOCK:Q_BLOCK + tq, :] = kc_ref[0]
    vbuf[0:Q_BLOCK, :] = vp_ref[0]
    vbuf[Q_BLOCK:Q_BLOCK + tq, :] = vc_ref[0]

    rows2 = HEADS_PER_VREG * Q_BLOCK
    keys = 2 * Q_BLOCK
    qi = lax.broadcasted_iota(jnp.int32, (rows2, keys), 0) & (Q_BLOCK - 1)
    kk = lax.broadcasted_iota(jnp.int32, (rows2, keys), 1)
    dist = qi + Q_BLOCK - kk
    in_window = (dist >= 0) & (dist <= Q_BLOCK)
    bias_ref[0] = jnp.where(in_window, 0.0, MASK_VALUE)
    bias_ref[1] = jnp.where(in_window & (kk >= Q_BLOCK), 0.0, MASK_VALUE)

    lane = lax.broadcasted_iota(jnp.int32, (Q_BLOCK, LANES), 1)
    head_a = lane < HEAD_DIM
    lse_group = lane // LSE_LANES_PER_HEAD
    ones_cols = jnp.ones((keys, LANES), BF16)

    def sub_block(j, carry):
        r0 = pl.multiple_of(j * Q_BLOCK, Q_BLOCK)
        at_start = jnp.logical_and(tile == 0, j == 0).astype(jnp.int32)
        bias = bias_ref[at_start]
        lse_tile = jnp.zeros((Q_BLOCK, LANES), F32)
        for hp in range(N_HEAD_PAIRS):
            cols = slice(hp * LANES, (hp + 1) * LANES)
            q2 = q_ref[0, pl.ds(r0, Q_BLOCK), cols]
            zero = jnp.zeros_like(q2)
            qs = jnp.concatenate([jnp.where(head_a, q2, zero), jnp.where(head_a, zero, q2)], axis=0)
            k2 = kbuf[pl.ds(r0, keys), cols]
            v2 = vbuf[pl.ds(r0, keys), cols]
            s = lax.dot_general(qs, k2, (((1,), (1,)), ((), ())), preferred_element_type=F32) + bias
            m = jnp.max(s, axis=1, keepdims=True)
            p = jnp.exp(s - m).astype(BF16)
            ov = jnp.dot(p, jnp.concatenate([v2, ones_cols], axis=1), preferred_element_type=F32)
            den = ov[:, LANES:]
            o = ov[:, :LANES] * (1.0 / den)
            o_ref[0, pl.ds(r0, Q_BLOCK), cols] = jnp.where(head_a, o[:Q_BLOCK], o[Q_BLOCK:]).astype(BF16)
            lse = m + jnp.log(den)
            lse_tile = jnp.where(lse_group == HEADS_PER_VREG * hp, lse[:Q_BLOCK], lse_tile)
            lse_tile = jnp.where(lse_group == HEADS_PER_VREG * hp + 1, lse[Q_BLOCK:], lse_tile)
        lse_ref[0, pl.ds(r0, Q_BLOCK), :] = lse_tile
        return carry

    lax.fori_loop(0, tq // Q_BLOCK, sub_block, 0)


def _attention_group(q, k, v, batch, seq, dilation, tq):
    sub_len = seq // dilation
    tq = min(tq, sub_len)
    view = lambda a, w: a.reshape(batch, sub_len, dilation * w)
    q3, k3, v3 = view(q, ATTN_WIDTH), view(k, ATTN_WIDTH), view(v, ATTN_WIDTH)
    prev_blocks_per_tile = tq // Q_BLOCK
    cur = lambda b, r, n: (b, n, r)
    prev = lambda b, r, n: (b, jnp.maximum(n * prev_blocks_per_tile - 1, 0), r)
    o3, lse3 = pl.pallas_call(
        _attn_kernel,
        out_shape=[jax.ShapeDtypeStruct(q3.shape, BF16),
                   jax.ShapeDtypeStruct((batch, sub_len, dilation * LANES), F32)],
        grid=(batch, dilation, sub_len // tq),
        in_specs=[
            pl.BlockSpec((1, tq, ATTN_WIDTH), cur),
            pl.BlockSpec((1, tq, ATTN_WIDTH), cur),
            pl.BlockSpec((1, Q_BLOCK, ATTN_WIDTH), prev),
            pl.BlockSpec((1, tq, ATTN_WIDTH), cur),
            pl.BlockSpec((1, Q_BLOCK, ATTN_WIDTH), prev),
        ],
        out_specs=[pl.BlockSpec((1, tq, ATTN_WIDTH), cur),
                   pl.BlockSpec((1, tq, LANES), cur)],
        scratch_shapes=[
            pltpu.VMEM((Q_BLOCK + tq, ATTN_WIDTH), BF16),
            pltpu.VMEM((Q_BLOCK + tq, ATTN_WIDTH), BF16),
            pltpu.VMEM((2, HEADS_PER_VREG * Q_BLOCK, 2 * Q_BLOCK), F32),
        ],
        compiler_params=pltpu.CompilerParams(
            dimension_semantics=("parallel", "parallel", "arbitrary"),
            vmem_limit_bytes=VMEM_LIMIT_BYTES),
        name=f"attn_d{dilation}",
    )(q3, k3, k3, v3, v3)
    tokens = batch * seq
    return o3.reshape(tokens, ATTN_WIDTH), lse3.reshape(tokens, LANES)


def _mix_out_kernel(o1_ref, o2_ref, o3_ref, l1_ref, l2_ref, l3_ref, c_ref, ga_ref, gc_ref, x_ref,
                    wa_ref, wc_ref, wo_ref, g_ref, b_ref, out_ref, *, alpha):
    lses = [l1_ref[...], l2_ref[...], l3_ref[...]]
    m = jnp.maximum(jnp.maximum(lses[0], lses[1]), lses[2])
    es = [jnp.exp(l - m) for l in lses]
    inv = 1.0 / (es[0] + es[1] + es[2])
    src = lax.broadcasted_iota(jnp.int32, (LANES, ATTN_WIDTH), 0)
    dst_head = lax.broadcasted_iota(jnp.int32, (LANES, ATTN_WIDTH), 1) // HEAD_DIM
    spread = jnp.where(src == dst_head * LSE_LANES_PER_HEAD, 1.0, 0.0).astype(BF16)
    o_mix = jnp.zeros(o1_ref.shape, F32)
    for e, o_ref in zip(es, (o1_ref, o2_ref, o3_ref)):
        w = e * inv
        w_hi = w.astype(BF16)
        w_lo = (w - w_hi.astype(F32)).astype(BF16)
        w_full = (jnp.dot(w_hi, spread, preferred_element_type=F32)
                  + jnp.dot(w_lo, spread, preferred_element_type=F32))
        o_mix = o_mix + w_full * o_ref[...].astype(F32)
    y_attn = jnp.dot(o_mix.astype(BF16), wa_ref[...], preferred_element_type=F32)
    y_conv = jnp.dot(c_ref[...], wc_ref[...], preferred_element_type=F32)
    merged = ga_ref[...].astype(F32) * y_attn + gc_ref[...].astype(F32) * y_conv
    mix = jnp.dot(merged.astype(BF16), wo_ref[...], preferred_element_type=F32)
    out_ref[...] = _layer_norm_rows(alpha * x_ref[...] + mix, g_ref[...], b_ref[...])


def _mix_out(os_, lses, c, ga, gc, x2, wa, wc, wo, ln_g, ln_b, alpha, tm):
    tokens = x2.shape[0]
    row = lambda i: (i, 0)
    const = lambda i: (0, 0)
    resident = lambda a: pl.BlockSpec(a.shape, const, pipeline_mode=pl.Buffered(1))
    return pl.pallas_call(
        functools.partial(_mix_out_kernel, alpha=alpha),
        out_shape=jax.ShapeDtypeStruct((tokens, D_MODEL), F32),
        grid=(tokens // tm,),
        in_specs=([pl.BlockSpec((tm, ATTN_WIDTH), row)] * 3 + [pl.BlockSpec((tm, LANES), row)] * 3
                  + [pl.BlockSpec((tm, CONV_CHANNELS), row)]
                  + [pl.BlockSpec((tm, D_MODEL), row)] * 3
                  + [resident(wa), resident(wc), resident(wo), resident(ln_g), resident(ln_b)]),
        out_specs=pl.BlockSpec((tm, D_MODEL), row),
        compiler_params=pltpu.CompilerParams(
            dimension_semantics=("parallel",), vmem_limit_bytes=VMEM_LIMIT_BYTES),
        name="mix_out",
    )(*os_, *lses, c, ga, gc, x2, wa, wc, wo, ln_g, ln_b)


def _ffn_kernel(x_ref, wg_ref, wu_ref, wd_ref, g_ref, b_ref, out_ref, h_ref, *, alpha, chunk):
    x = x_ref[...]
    xb = x.astype(BF16)
    d_ff = wg_ref.shape[1]
    for c0 in range(0, d_ff, chunk):
        gate = jnp.dot(xb, wg_ref[:, c0:c0 + chunk], preferred_element_type=F32)
        up = jnp.dot(xb, wu_ref[:, c0:c0 + chunk], preferred_element_type=F32)
        h_ref[:, c0:c0 + chunk] = (gate * jax.nn.sigmoid(gate) * up).astype(BF16)
    ff = jnp.dot(h_ref[...], wd_ref[...], preferred_element_type=F32)
    out_ref[...] = _layer_norm_rows(alpha * x + ff, g_ref[...], b_ref[...])


def _ffn(x1, wg, wu, wd, ln_g, ln_b, alpha, tm):
    tokens = x1.shape[0]
    d_ff = wg.shape[1]
    row = lambda i: (i, 0)
    const = lambda i: (0, 0)
    resident = lambda a: pl.BlockSpec(a.shape, const, pipeline_mode=pl.Buffered(1))
    return pl.pallas_call(
        functools.partial(_ffn_kernel, alpha=alpha, chunk=256),
        out_shape=jax.ShapeDtypeStruct((tokens, D_MODEL), F32),
        grid=(tokens // tm,),
        in_specs=[pl.BlockSpec((tm, D_MODEL), row),
                  resident(wg), resident(wu), resident(wd), resident(ln_g), resident(ln_b)],
        out_specs=pl.BlockSpec((tm, D_MODEL), row),
        scratch_shapes=[pltpu.VMEM((tm, d_ff), BF16)],
        compiler_params=pltpu.CompilerParams(
            dimension_semantics=("parallel",), vmem_limit_bytes=VMEM_LIMIT_BYTES),
        name="ffn",
    )(x1, wg, wu, wd, ln_g, ln_b)


def _rotary_tables(seq):
    inv_freq = 1.0 / (ROPE_THETA ** (jnp.arange(0, HEAD_DIM, 2, dtype=F32) / HEAD_DIM))
    ang = jnp.arange(seq, dtype=F32)[:, None] * inv_freq[None, :]
    cos, sin = jnp.cos(ang), jnp.sin(ang)
    cos_t = jnp.concatenate([cos, cos] * HEADS_PER_VREG, axis=1)
    sin_t = jnp.concatenate([-sin, sin] * HEADS_PER_VREG, axis=1)
    return cos_t, sin_t


def kernel(x, w_in, conv_w, conv_b, conv_ln_g, conv_ln_b, w_attn_out, w_conv_out, w_o, ln1_g, ln1_b,
           w_ffn_gate, w_ffn_up, w_ffn_down, ln2_g, ln2_b):
    batch, seq, d_model = x.shape
    depth = w_in.shape[0]
    alpha = (2 * depth) ** 0.25
    tokens = batch * seq
    cos_t, sin_t = _rotary_tables(seq)
    vec = lambda a: a.reshape(1, -1)
    h = x.reshape(tokens, d_model)
    for l in range(depth):
        outs = _in_proj(h, w_in[l].astype(BF16), cos_t, sin_t, seq, tm=512)
        qkv, (u, ga, gc) = outs[:3 * N_GROUPS], outs[3 * N_GROUPS:]
        c = _conv_branch(u.reshape(batch, seq, CONV_CHANNELS), conv_w[l], vec(conv_b[l]),
                         vec(conv_ln_g[l]), vec(conv_ln_b[l]), tc=512).reshape(tokens, CONV_CHANNELS)
        os_, lses = [], []
        for g, (_, dilation) in enumerate(DILATED_GROUPS):
            o, lse = _attention_group(qkv[3 * g], qkv[3 * g + 1], qkv[3 * g + 2], batch, seq, dilation, tq=512)
            os_.append(o)
            lses.append(lse)
        h1 = _mix_out(os_, lses, c, ga, gc, h, w_attn_out[l].astype(BF16), w_conv_out[l].astype(BF16),
                      w_o[l].astype(BF16), vec(ln1_g[l]), vec(ln1_b[l]), alpha, tm=512)
        h = _ffn(h1, w_ffn_gate[l].astype(BF16), w_ffn_up[l].astype(BF16), w_ffn_down[l].astype(BF16),
                 vec(ln2_g[l]), vec(ln2_b[l]), alpha, tm=512)
    return h.reshape(batch, seq, d_model)
```

```python
import functools

import jax
import jax.numpy as jnp
from jax import lax
from jax.experimental import pallas as pl
from jax.experimental.pallas import tpu as pltpu

D_MODEL = 1024
HEAD_DIM = 64
N_HEADS = 8
ATTN_WIDTH = N_HEADS * HEAD_DIM
DILATED_GROUPS = ((128, 1), (512, 4), (2048, 16))
N_GROUPS = len(DILATED_GROUPS)
CONV_CHANNELS = 512
CONV_WIDTH = 31
ROPE_THETA = 10000.0
LN_EPS = 1e-5
QKV_COLS = N_GROUPS * 3 * ATTN_WIDTH
CONV_COL0 = QKV_COLS
GATE_COL0 = QKV_COLS + 2 * CONV_CHANNELS

LANES = 128
HEADS_PER_VREG = LANES // HEAD_DIM
N_HEAD_PAIRS = N_HEADS // HEADS_PER_VREG
LSE_LANES_PER_HEAD = LANES // N_HEADS
Q_BLOCK = 128
CONV_HALO = 32
MASK_VALUE = -1e30
VMEM_LIMIT_BYTES = 56 * 1024 * 1024

F32 = jnp.float32
BF16 = jnp.bfloat16

assert all(window // dilation == Q_BLOCK for window, dilation in DILATED_GROUPS)


def _layer_norm_rows(h, g, b):
    mu = jnp.mean(h, axis=-1, keepdims=True)
    d = h - mu
    var = jnp.mean(d * d, axis=-1, keepdims=True)
    return d * lax.rsqrt(var + LN_EPS) * g + b


def _in_proj_kernel(x_ref, w_ref, cos_ref, sin_ref, *rest):
    out_refs, slab_ref = rest[:-1], rest[-1]
    qkv_refs = out_refs[:3 * N_GROUPS]
    u_ref, ga_ref, gc_ref = out_refs[3 * N_GROUPS:]
    tm = x_ref.shape[0]
    n_slabs = x_ref.shape[1] // LANES
    xb = x_ref[...].astype(BF16)
    for c in range(n_slabs):
        slab_ref[c] = x_ref[:, c * LANES:(c + 1) * LANES]
    lane = lax.broadcasted_iota(jnp.int32, (tm, LANES), 1)
    first_half = (lane & (HEAD_DIM // 2)) == 0

    def residue_major(d):
        if d == 1:
            return xb
        rows = [jnp.concatenate([slab_ref[c, pl.ds(r, tm // d, stride=d), :] for c in range(n_slabs)], axis=1)
                for r in range(d)]
        return jnp.concatenate(rows, axis=0).astype(BF16)

    def mm(lhs, col0, width):
        return jnp.dot(lhs, w_ref[:, col0:col0 + width], preferred_element_type=F32)

    def rope(h, cos_t, sin_t):
        outs = []
        for c in range(h.shape[1] // LANES):
            hs = h[:, c * LANES:(c + 1) * LANES]
            partner = jnp.where(first_half,
                                pltpu.roll(hs, LANES - HEAD_DIM // 2, 1),
                                pltpu.roll(hs, HEAD_DIM // 2, 1))
            outs.append(hs * cos_t + partner * sin_t)
        return jnp.concatenate(outs, axis=1)

    def store_residue_view(ref, h, d):
        rows = tm // d
        for r in range(d):
            ref[0, :, r * ATTN_WIDTH:(r + 1) * ATTN_WIDTH] = h[r * rows:(r + 1) * rows, :].astype(BF16)

    for g, (_, d) in enumerate(DILATED_GROUPS):
        base = g * 3 * ATTN_WIDTH
        xg = residue_major(d)
        cos_t = cos_ref[g]
        sin_t = sin_ref[g]
        q = rope(mm(xg, base, ATTN_WIDTH), cos_t, sin_t) * (HEAD_DIM ** -0.5)
        store_residue_view(qkv_refs[3 * g], q, d)
        store_residue_view(qkv_refs[3 * g + 1], rope(mm(xg, base + ATTN_WIDTH, ATTN_WIDTH), cos_t, sin_t), d)
        store_residue_view(qkv_refs[3 * g + 2], mm(xg, base + 2 * ATTN_WIDTH, ATTN_WIDTH), d)

    a = mm(xb, CONV_COL0, CONV_CHANNELS)
    b = mm(xb, CONV_COL0 + CONV_CHANNELS, CONV_CHANNELS)
    u_ref[...] = (a * jax.nn.sigmoid(b)).astype(BF16)
    ga_ref[...] = jax.nn.sigmoid(mm(xb, GATE_COL0, D_MODEL)).astype(BF16)
    gc_ref[...] = jax.nn.sigmoid(mm(xb, GATE_COL0 + D_MODEL, D_MODEL)).astype(BF16)


def _in_proj(x2, w_in_b, cos_t, sin_t, batch, seq, tm):
    tokens = x2.shape[0]
    n_seq_tiles = seq // tm
    row = lambda i: (i, 0)
    tab = lambda i: (0, i % n_seq_tiles, 0)
    view = lambda i: (i // n_seq_tiles, i % n_seq_tiles, 0)
    qkv_shapes, qkv_specs = [], []
    for _, d in DILATED_GROUPS:
        qkv_shapes += [jax.ShapeDtypeStruct((batch, seq // d, d * ATTN_WIDTH), BF16)] * 3
        qkv_specs += [pl.BlockSpec((1, tm // d, d * ATTN_WIDTH), view)] * 3
    out_shape = (qkv_shapes
                 + [jax.ShapeDtypeStruct((tokens, CONV_CHANNELS), BF16)]
                 + [jax.ShapeDtypeStruct((tokens, D_MODEL), BF16)] * 2)
    out_specs = (qkv_specs
                 + [pl.BlockSpec((tm, CONV_CHANNELS), row)]
                 + [pl.BlockSpec((tm, D_MODEL), row)] * 2)
    return pl.pallas_call(
        _in_proj_kernel,
        out_shape=out_shape,
        grid=(tokens // tm,),
        in_specs=[
            pl.BlockSpec((tm, D_MODEL), row),
            pl.BlockSpec(w_in_b.shape, lambda i: (0, 0), pipeline_mode=pl.Buffered(1)),
            pl.BlockSpec((N_GROUPS, tm, LANES), tab),
            pl.BlockSpec((N_GROUPS, tm, LANES), tab),
        ],
        out_specs=out_specs,
        scratch_shapes=[pltpu.VMEM((D_MODEL // LANES, tm, LANES), F32)],
        compiler_params=pltpu.CompilerParams(
            dimension_semantics=("parallel",), vmem_limit_bytes=VMEM_LIMIT_BYTES),
        name="in_proj",
    )(x2, w_in_b, cos_t, sin_t)


def _conv_kernel(u_ref, halo_ref, w_ref, b_ref, g_ref, beta_ref, o_ref, win_ref, *, rows_per_chunk):
    tc = u_ref.shape[1]
    halo = halo_ref[0].astype(F32)
    halo = jnp.where(pl.program_id(1) > 0, halo, 0.0)
    win_ref[0:CONV_HALO, :] = halo
    win_ref[CONV_HALO:CONV_HALO + tc, :] = u_ref[0].astype(F32)
    first_tap_row = CONV_HALO - (CONV_WIDTH - 1)
    bias = b_ref[...]
    gain = g_ref[...]
    beta = beta_ref[...]
    for r0 in range(0, tc, rows_per_chunk):
        acc = jnp.zeros((rows_per_chunk, CONV_CHANNELS), F32)
        for j in range(CONV_WIDTH):
            start = r0 + first_tap_row + j
            acc = acc + win_ref[start:start + rows_per_chunk, :] * w_ref[j:j + 1, :]
        y = _layer_norm_rows(acc + bias, gain, beta)
        o_ref[0, r0:r0 + rows_per_chunk, :] = (y * jax.nn.sigmoid(y)).astype(BF16)


def _conv_branch(u3, conv_w, conv_b, ln_g, ln_b, tc):
    batch, seq, _ = u3.shape
    halo_blocks_per_tile = tc // CONV_HALO
    vec = pl.BlockSpec((1, CONV_CHANNELS), lambda b, i: (0, 0))
    return pl.pallas_call(
        functools.partial(_conv_kernel, rows_per_chunk=64),
        out_shape=jax.ShapeDtypeStruct(u3.shape, BF16),
        grid=(batch, seq // tc),
        in_specs=[
            pl.BlockSpec((1, tc, CONV_CHANNELS), lambda b, i: (b, i, 0)),
            pl.BlockSpec((1, CONV_HALO, CONV_CHANNELS),
                         lambda b, i: (b, jnp.maximum(i * halo_blocks_per_tile - 1, 0), 0)),
            pl.BlockSpec((CONV_WIDTH, CONV_CHANNELS), lambda b, i: (0, 0)),
            vec, vec, vec,
        ],
        out_specs=pl.BlockSpec((1, tc, CONV_CHANNELS), lambda b, i: (b, i, 0)),
        scratch_shapes=[pltpu.VMEM((CONV_HALO + tc, CONV_CHANNELS), F32)],
        compiler_params=pltpu.CompilerParams(
            dimension_semantics=("parallel", "parallel"), vmem_limit_bytes=VMEM_LIMIT_BYTES),
        name="conv_branch",
    )(u3, u3, conv_w, conv_b, ln_g, ln_b)


def _attn_kernel(q_ref, kc_ref, kp_ref, vc_ref, vp_ref, o_ref, lse_ref, kbuf, vbuf, bias_ref):
    tq = q_ref.shape[1]
    tile = pl.program_id(2)
    kbuf[0:Q_BLOCK, :] = kp_ref[0]
    kbuf[Q_BLOCK:Q_BLOCK + tq, :] = kc_ref[0]
    vbuf[0:Q_BLOCK, :] = vp_ref[0]
    vbuf[Q_BLOCK:Q_BLOCK + tq, :] = vc_ref[0]

    rows2 = HEADS_PER_VREG * Q_BLOCK
    keys = 2 * Q_BLOCK
    qi = lax.broadcasted_iota(jnp.int32, (rows2, keys), 0) & (Q_BLOCK - 1)
    kk = lax.broadcasted_iota(jnp.int32, (rows2, keys), 1)
    dist = qi + Q_BLOCK - kk
    in_window = (dist >= 0) & (dist <= Q_BLOCK)
    bias_ref[0] = jnp.where(in_window, 0.0, MASK_VALUE)
    bias_ref[1] = jnp.where(in_window & (kk >= Q_BLOCK), 0.0, MASK_VALUE)

    lane = lax.broadcasted_iota(jnp.int32, (Q_BLOCK, LANES), 1)
    head_a = lane < HEAD_DIM
    lse_group = lane // LSE_LANES_PER_HEAD
    ones_cols = jnp.ones((keys, LANES), BF16)

    def sub_block(j, carry):
        r0 = pl.multiple_of(j * Q_BLOCK, Q_BLOCK)
        at_start = jnp.logical_and(tile == 0, j == 0).astype(jnp.int32)
        bias = bias_ref[at_start]
        lse_tile = jnp.zeros((Q_BLOCK, LANES), F32)
        for hp in range(N_HEAD_PAIRS):
            cols = slice(hp * LANES, (hp + 1) * LANES)
            q2 = q_ref[0, pl.ds(r0, Q_BLOCK), cols]
            zero = jnp.zeros_like(q2)
            qs = jnp.concatenate([jnp.where(head_a, q2, zero), jnp.where(head_a, zero, q2)], axis=0)
            k2 = kbuf[pl.ds(r0, keys), cols]
            v2 = vbuf[pl.ds(r0, keys), cols]
            s = lax.dot_general(qs, k2, (((1,), (1,)), ((), ())), preferred_element_type=F32) + bias
            m = jnp.max(s, axis=1, keepdims=True)
            p = jnp.exp(s - m).astype(BF16)
            ov = jnp.dot(p, jnp.concatenate([v2, ones_cols], axis=1), preferred_element_type=F32)
            den = ov[:, LANES:]
            o = ov[:, :LANES] * (1.0 / den)
            o_ref[0, pl.ds(r0, Q_BLOCK), cols] = jnp.where(head_a, o[:Q_BLOCK], o[Q_BLOCK:]).astype(BF16)
            lse = m + jnp.log(den)
            lse_tile = jnp.where(lse_group == HEADS_PER_VREG * hp, lse[:Q_BLOCK], lse_tile)
            lse_tile = jnp.where(lse_group == HEADS_PER_VREG * hp + 1, lse[Q_BLOCK:], lse_tile)
        lse_ref[0, pl.ds(r0, Q_BLOCK), :] = lse_tile
        return carry

    lax.fori_loop(0, tq // Q_BLOCK, sub_block, 0)


def _attention_group(q3, k3, v3, dilation, tq):
    batch, sub_len, _ = q3.shape
    tq = min(tq, sub_len)
    prev_blocks_per_tile = tq // Q_BLOCK
    cur = lambda b, r, n: (b, n, r)
    prev = lambda b, r, n: (b, jnp.maximum(n * prev_blocks_per_tile - 1, 0), r)
    return pl.pallas_call(
        _attn_kernel,
        out_shape=[jax.ShapeDtypeStruct(q3.shape, BF16),
                   jax.ShapeDtypeStruct((batch, sub_len, dilation * LANES), F32)],
        grid=(batch, dilation, sub_len // tq),
        in_specs=[
            pl.BlockSpec((1, tq, ATTN_WIDTH), cur),
            pl.BlockSpec((1, tq, ATTN_WIDTH), cur),
            pl.BlockSpec((1, Q_BLOCK, ATTN_WIDTH), prev),
            pl.BlockSpec((1, tq, ATTN_WIDTH), cur),
            pl.BlockSpec((1, Q_BLOCK, ATTN_WIDTH), prev),
        ],
        out_specs=[pl.BlockSpec((1, tq, ATTN_WIDTH), cur),
                   pl.BlockSpec((1, tq, LANES), cur)],
        scratch_shapes=[
            pltpu.VMEM((Q_BLOCK + tq, ATTN_WIDTH), BF16),
            pltpu.VMEM((Q_BLOCK + tq, ATTN_WIDTH), BF16),
            pltpu.VMEM((2, HEADS_PER_VREG * Q_BLOCK, 2 * Q_BLOCK), F32),
        ],
        compiler_params=pltpu.CompilerParams(
            dimension_semantics=("parallel", "parallel", "arbitrary"),
            vmem_limit_bytes=VMEM_LIMIT_BYTES),
        name=f"attn_d{dilation}",
    )(q3, k3, k3, v3, v3)


def _mix_out_kernel(o1_ref, o2_ref, o3_ref, l1_ref, l2_ref, l3_ref, c_ref, ga_ref, gc_ref, x_ref,
                    wa_ref, wc_ref, wo_ref, g_ref, b_ref, out_ref, o_nat_ref, l_nat_ref, *, alpha):
    tm = x_ref.shape[0]
    n_slabs = ATTN_WIDTH // LANES

    def token_major(g, o_ref, l_ref, d):
        if d == 1:
            return o_ref[0].astype(F32), l_ref[0]
        for r in range(d):
            rows = pl.ds(r, tm // d, stride=d)
            piece = o_ref[0, :, r * ATTN_WIDTH:(r + 1) * ATTN_WIDTH].astype(F32)
            for c in range(n_slabs):
                o_nat_ref[g, c, rows, :] = piece[:, c * LANES:(c + 1) * LANES]
            l_nat_ref[g, rows, :] = l_ref[0, :, r * LANES:(r + 1) * LANES]
        return jnp.concatenate([o_nat_ref[g, c] for c in range(n_slabs)], axis=1), l_nat_ref[g]

    os_, lses = [], []
    for g, (o_ref, l_ref) in enumerate(((o1_ref, l1_ref), (o2_ref, l2_ref), (o3_ref, l3_ref))):
        o, l = token_major(g, o_ref, l_ref, DILATED_GROUPS[g][1])
        os_.append(o)
        lses.append(l)
    m = jnp.maximum(jnp.maximum(lses[0], lses[1]), lses[2])
    es = [jnp.exp(l - m) for l in lses]
    inv = 1.0 / (es[0] + es[1] + es[2])
    src = lax.broadcasted_iota(jnp.int32, (LANES, ATTN_WIDTH), 0)
    dst_head = lax.broadcasted_iota(jnp.int32, (LANES, ATTN_WIDTH), 1) // HEAD_DIM
    spread = jnp.where(src == dst_head * LSE_LANES_PER_HEAD, 1.0, 0.0).astype(BF16)
    o_mix = jnp.zeros((tm, ATTN_WIDTH), F32)
    for e, o in zip(es, os_):
        w = e * inv
        w_hi = w.astype(BF16)
        w_lo = (w - w_hi.astype(F32)).astype(BF16)
        w_full = (jnp.dot(w_hi, spread, preferred_element_type=F32)
                  + jnp.dot(w_lo, spread, preferred_element_type=F32))
        o_mix = o_mix + w_full * o
    y_attn = jnp.dot(o_mix.astype(BF16), wa_ref[...], preferred_element_type=F32)
    y_conv = jnp.dot(c_ref[...], wc_ref[...], preferred_element_type=F32)
    merged = ga_ref[...].astype(F32) * y_attn + gc_ref[...].astype(F32) * y_conv
    mix = jnp.dot(merged.astype(BF16), wo_ref[...], preferred_element_type=F32)
    out_ref[...] = _layer_norm_rows(alpha * x_ref[...] + mix, g_ref[...], b_ref[...])


def _mix_out(os_, lses, c, ga, gc, x2, wa, wc, wo, ln_g, ln_b, alpha, seq, tm):
    tokens = x2.shape[0]
    n_seq_tiles = seq // tm
    row = lambda i: (i, 0)
    view = lambda i: (i // n_seq_tiles, i % n_seq_tiles, 0)
    const = lambda i: (0, 0)
    resident = lambda a: pl.BlockSpec(a.shape, const, pipeline_mode=pl.Buffered(1))
    o_specs = [pl.BlockSpec((1, tm // d, d * ATTN_WIDTH), view) for _, d in DILATED_GROUPS]
    l_specs = [pl.BlockSpec((1, tm // d, d * LANES), view) for _, d in DILATED_GROUPS]
    return pl.pallas_call(
        functools.partial(_mix_out_kernel, alpha=alpha),
        out_shape=jax.ShapeDtypeStruct((tokens, D_MODEL), F32),
        grid=(tokens // tm,),
        in_specs=(o_specs + l_specs
                  + [pl.BlockSpec((tm, CONV_CHANNELS), row)]
                  + [pl.BlockSpec((tm, D_MODEL), row)] * 3
                  + [resident(wa), resident(wc), resident(wo), resident(ln_g), resident(ln_b)]),
        out_specs=pl.BlockSpec((tm, D_MODEL), row),
        scratch_shapes=[pltpu.VMEM((N_GROUPS, ATTN_WIDTH // LANES, tm, LANES), F32),
                        pltpu.VMEM((N_GROUPS, tm, LANES), F32)],
        compiler_params=pltpu.CompilerParams(
            dimension_semantics=("parallel",), vmem_limit_bytes=VMEM_LIMIT_BYTES),
        name="mix_out",
    )(*os_, *lses, c, ga, gc, x2, wa, wc, wo, ln_g, ln_b)


def _ffn_kernel(x_ref, wg_ref, wu_ref, wd_ref, g_ref, b_ref, out_ref, h_ref, *, alpha, chunk):
    x = x_ref[...]
    xb = x.astype(BF16)
    d_ff = wg_ref.shape[1]
    for c0 in range(0, d_ff, chunk):
        gate = jnp.dot(xb, wg_ref[:, c0:c0 + chunk], preferred_element_type=F32)
        up = jnp.dot(xb, wu_ref[:, c0:c0 + chunk], preferred_element_type=F32)
        h_ref[:, c0:c0 + chunk] = (gate * jax.nn.sigmoid(gate) * up).astype(BF16)
    ff = jnp.dot(h_ref[...], wd_ref[...], preferred_element_type=F32)
    out_ref[...] = _layer_norm_rows(alpha * x + ff, g_ref[...], b_ref[...])


def _ffn(x1, wg, wu, wd, ln_g, ln_b, alpha, tm):
    tokens = x1.shape[0]
    d_ff = wg.shape[1]
    row = lambda i: (i, 0)
    const = lambda i: (0, 0)
    resident = lambda a: pl.BlockSpec(a.shape, const, pipeline_mode=pl.Buffered(1))
    return pl.pallas_call(
        functools.partial(_ffn_kernel, alpha=alpha, chunk=256),
        out_shape=jax.ShapeDtypeStruct((tokens, D_MODEL), F32),
        grid=(tokens // tm,),
        in_specs=[pl.BlockSpec((tm, D_MODEL), row),
                  resident(wg), resident(wu), resident(wd), resident(ln_g), resident(ln_b)],
        out_specs=pl.BlockSpec((tm, D_MODEL), row),
        scratch_shapes=[pltpu.VMEM((tm, d_ff), BF16)],
        compiler_params=pltpu.CompilerParams(
            dimension_semantics=("parallel",), vmem_limit_bytes=VMEM_LIMIT_BYTES),
        name="ffn",
    )(x1, wg, wu, wd, ln_g, ln_b)


def _rotary_tables(seq, tm):
    inv_freq = 1.0 / (ROPE_THETA ** (jnp.arange(0, HEAD_DIM, 2, dtype=F32) / HEAD_DIM))
    ang = jnp.arange(seq, dtype=F32)[:, None] * inv_freq[None, :]
    cos, sin = jnp.cos(ang), jnp.sin(ang)
    cos_t = jnp.concatenate([cos, cos] * HEADS_PER_VREG, axis=1)
    sin_t = jnp.concatenate([-sin, sin] * HEADS_PER_VREG, axis=1)

    def group_order(t, d):
        return t.reshape(seq // tm, tm // d, d, LANES).transpose(0, 2, 1, 3).reshape(seq, LANES)

    return (jnp.stack([group_order(cos_t, d) for _, d in DILATED_GROUPS]),
            jnp.stack([group_order(sin_t, d) for _, d in DILATED_GROUPS]))


def kernel(x, w_in, conv_w, conv_b, conv_ln_g, conv_ln_b, w_attn_out, w_conv_out, w_o, ln1_g, ln1_b,
           w_ffn_gate, w_ffn_up, w_ffn_down, ln2_g, ln2_b):
    batch, seq, d_model = x.shape
    depth = w_in.shape[0]
    alpha = (2 * depth) ** 0.25
    tokens = batch * seq
    tm = 512
    cos_t, sin_t = _rotary_tables(seq, tm)
    vec = lambda a: a.reshape(1, -1)
    h = x.reshape(tokens, d_model)
    for l in range(depth):
        outs = _in_proj(h, w_in[l].astype(BF16), cos_t, sin_t, batch, seq, tm)
        qkv, (u, ga, gc) = outs[:3 * N_GROUPS], outs[3 * N_GROUPS:]
        c = _conv_branch(u.reshape(batch, seq, CONV_CHANNELS), conv_w[l], vec(conv_b[l]),
                         vec(conv_ln_g[l]), vec(conv_ln_b[l]), tc=512).reshape(tokens, CONV_CHANNELS)
        os_, lses = [], []
        for g, (_, dilation) in enumerate(DILATED_GROUPS):
            o, lse = _attention_group(qkv[3 * g], qkv[3 * g + 1], qkv[3 * g + 2], dilation, tq=512)
            os_.append(o)
            lses.append(lse)
        h1 = _mix_out(os_, lses, c, ga, gc, h, w_attn_out[l].astype(BF16), w_conv_out[l].astype(BF16),
                      w_o[l].astype(BF16), vec(ln1_g[l]), vec(ln1_b[l]), alpha, seq, tm)
        h = _ffn(h1, w_ffn_gate[l].astype(BF16), w_ffn_up[l].astype(BF16), w_ffn_down[l].astype(BF16),
                 vec(ln2_g[l]), vec(ln2_b[l]), alpha, tm)
    return h.reshape(batch, seq, d_model)
```

```python
import functools

import jax
import jax.numpy as jnp
from jax import lax
from jax.experimental import pallas as pl
from jax.experimental.pallas import tpu as pltpu

D_MODEL = 1024
HEAD_DIM = 64
N_HEADS = 8
ATTN_WIDTH = N_HEADS * HEAD_DIM
DILATED_GROUPS = ((128, 1), (512, 4), (2048, 16))
N_GROUPS = len(DILATED_GROUPS)
CONV_CHANNELS = 512
CONV_WIDTH = 31
ROPE_THETA = 10000.0
LN_EPS = 1e-5
QKV_COLS = N_GROUPS * 3 * ATTN_WIDTH
CONV_COL0 = QKV_COLS
GATE_COL0 = QKV_COLS + 2 * CONV_CHANNELS

LANES = 128
SUBLANES = 8
HEADS_PER_VREG = LANES // HEAD_DIM
N_HEAD_PAIRS = N_HEADS // HEADS_PER_VREG
LSE_LANES_PER_HEAD = LANES // N_HEADS
Q_BLOCK = 128
CONV_HALO = 32
MASK_VALUE = -1e30
VMEM_LIMIT_BYTES = 56 * 1024 * 1024

F32 = jnp.float32
BF16 = jnp.bfloat16

assert all(window // dilation == Q_BLOCK for window, dilation in DILATED_GROUPS)


def _layer_norm_rows(h, g, b):
    mu = jnp.mean(h, axis=-1, keepdims=True)
    d = h - mu
    var = jnp.mean(d * d, axis=-1, keepdims=True)
    return d * lax.rsqrt(var + LN_EPS) * g + b


def _in_proj_kernel(x_ref, xh_ref, w_ref, cos_ref, sin_ref, cw_ref, cb_ref, cg_ref, cbeta_ref, *rest,
                    n_seq_tiles):
    out_refs, (slab_ref, ext_ref, cnat_ref) = rest[:-3], rest[-3:]
    qkv_refs = out_refs[:3 * N_GROUPS]
    c_ref, ga_ref, gc_ref = out_refs[3 * N_GROUPS:]
    tm = x_ref.shape[0]
    n_slabs = x_ref.shape[1] // LANES
    xb = x_ref[...].astype(BF16)
    first_tile = (pl.program_id(0) % n_seq_tiles) == 0
    halo = jnp.where(first_tile, 0.0, xh_ref[...])
    for c in range(n_slabs):
        slab_ref[c, 0:CONV_HALO, :] = halo[:, c * LANES:(c + 1) * LANES]
        slab_ref[c, CONV_HALO:CONV_HALO + tm, :] = x_ref[:, c * LANES:(c + 1) * LANES]
    lane = lax.broadcasted_iota(jnp.int32, (tm, LANES), 1)
    first_half = (lane & (HEAD_DIM // 2)) == 0

    def residue_major(d):
        if d == 1:
            return xb
        rows = [jnp.concatenate([slab_ref[c, pl.ds(CONV_HALO + r, tm // d, stride=d), :]
                                 for c in range(n_slabs)], axis=1) for r in range(d)]
        return jnp.concatenate(rows, axis=0).astype(BF16)

    def table_rows(ref, d):
        if d == 1:
            return ref[...]
        return jnp.concatenate([ref[pl.ds(r, tm // d, stride=d), :] for r in range(d)], axis=0)

    def mm(lhs, col0, width):
        return jnp.dot(lhs, w_ref[:, col0:col0 + width], preferred_element_type=F32)

    window = CONV_HALO + tm
    pitch = window // SUBLANES
    x_win = jnp.concatenate(
        [jnp.concatenate([slab_ref[c, pl.ds(a, SUBLANES, stride=pitch), :] for c in range(n_slabs)], axis=1)
         for a in range(pitch)], axis=0).astype(BF16)
    u = mm(x_win, CONV_COL0, CONV_CHANNELS) * jax.nn.sigmoid(mm(x_win, CONV_COL0 + CONV_CHANNELS, CONV_CHANNELS))
    n_wrap = CONV_WIDTH - 1
    ext_ref[n_wrap * SUBLANES:, :] = u
    ext_ref[0:n_wrap * SUBLANES, :] = ext_ref[pitch * SUBLANES - 1:(pitch + n_wrap) * SUBLANES - 1, :]
    conv_bias, conv_gain, conv_beta = cb_ref[...], cg_ref[...], cbeta_ref[...]
    vregs_per_chunk = 4
    for a0 in range(0, pitch, vregs_per_chunk):
        acc = jnp.zeros((vregs_per_chunk * SUBLANES, CONV_CHANNELS), F32)
        for k in range(CONV_WIDTH):
            e0 = (n_wrap + a0 - k) * SUBLANES
            acc = acc + ext_ref[e0:e0 + vregs_per_chunk * SUBLANES, :] * cw_ref[n_wrap - k:n_wrap - k + 1, :]
        y = _layer_norm_rows(acc + conv_bias, conv_gain, conv_beta)
        y = y * jax.nn.sigmoid(y)
        for i in range(vregs_per_chunk):
            for c in range(CONV_CHANNELS // LANES):
                cnat_ref[c, pl.ds(a0 + i, SUBLANES, stride=pitch), :] = (
                    y[i * SUBLANES:(i + 1) * SUBLANES, c * LANES:(c + 1) * LANES])
    c_ref[...] = jnp.concatenate([cnat_ref[c, CONV_HALO:CONV_HALO + tm, :]
                                  for c in range(CONV_CHANNELS // LANES)], axis=1).astype(BF16)

    def rope(h, cos_t, sin_t):
        outs = []
        for c in range(h.shape[1] // LANES):
            hs = h[:, c * LANES:(c + 1) * LANES]
            partner = jnp.where(first_half,
                                pltpu.roll(hs, LANES - HEAD_DIM // 2, 1),
                                pltpu.roll(hs, HEAD_DIM // 2, 1))
            outs.append(hs * cos_t + partner * sin_t)
        return jnp.concatenate(outs, axis=1)

    def store_residue_view(ref, h, d):
        rows = tm // d
        for r in range(d):
            ref[0, :, r * ATTN_WIDTH:(r + 1) * ATTN_WIDTH] = h[r * rows:(r + 1) * rows, :].astype(BF16)

    for g, (_, d) in enumerate(DILATED_GROUPS):
        base = g * 3 * ATTN_WIDTH
        xg = residue_major(d)
        cos_t = table_rows(cos_ref, d)
        sin_t = table_rows(sin_ref, d)
        q = rope(mm(xg, base, ATTN_WIDTH), cos_t, sin_t) * (HEAD_DIM ** -0.5)
        store_residue_view(qkv_refs[3 * g], q, d)
        store_residue_view(qkv_refs[3 * g + 1], rope(mm(xg, base + ATTN_WIDTH, ATTN_WIDTH), cos_t, sin_t), d)
        store_residue_view(qkv_refs[3 * g + 2], mm(xg, base + 2 * ATTN_WIDTH, ATTN_WIDTH), d)

    ga_ref[...] = jax.nn.sigmoid(mm(xb, GATE_COL0, D_MODEL)).astype(BF16)
    gc_ref[...] = jax.nn.sigmoid(mm(xb, GATE_COL0 + D_MODEL, D_MODEL)).astype(BF16)


def _in_proj(x2, w_in_b, cos_t, sin_t, conv_w, conv_b, conv_ln_g, conv_ln_b, batch, seq, tm):
    tokens = x2.shape[0]
    n_seq_tiles = seq // tm
    halo_blocks_per_tile = tm // CONV_HALO
    window = CONV_HALO + tm
    assert window % SUBLANES == 0 and (window // SUBLANES) % 8 != 0
    row = lambda i: (i, 0)
    const = lambda i: (0, 0)
    tab = lambda i: (i % n_seq_tiles, 0)
    view = lambda i: (i // n_seq_tiles, i % n_seq_tiles, 0)
    vec = pl.BlockSpec((1, CONV_CHANNELS), const)
    qkv_shapes, qkv_specs = [], []
    for _, d in DILATED_GROUPS:
        qkv_shapes += [jax.ShapeDtypeStruct((batch, seq // d, d * ATTN_WIDTH), BF16)] * 3
        qkv_specs += [pl.BlockSpec((1, tm // d, d * ATTN_WIDTH), view)] * 3
    out_shape = (qkv_shapes
                 + [jax.ShapeDtypeStruct((tokens, CONV_CHANNELS), BF16)]
                 + [jax.ShapeDtypeStruct((tokens, D_MODEL), BF16)] * 2)
    out_specs = (qkv_specs
                 + [pl.BlockSpec((tm, CONV_CHANNELS), row)]
                 + [pl.BlockSpec((tm, D_MODEL), row)] * 2)
    return pl.pallas_call(
        functools.partial(_in_proj_kernel, n_seq_tiles=n_seq_tiles),
        out_shape=out_shape,
        grid=(tokens // tm,),
        in_specs=[
            pl.BlockSpec((tm, D_MODEL), row),
            pl.BlockSpec((CONV_HALO, D_MODEL), lambda i: (jnp.maximum(i * halo_blocks_per_tile - 1, 0), 0)),
            pl.BlockSpec(w_in_b.shape, const, pipeline_mode=pl.Buffered(1)),
            pl.BlockSpec((tm, LANES), tab),
            pl.BlockSpec((tm, LANES), tab),
            pl.BlockSpec((CONV_WIDTH, CONV_CHANNELS), const),
            vec, vec, vec,
        ],
        out_specs=out_specs,
        scratch_shapes=[pltpu.VMEM((D_MODEL // LANES, window, LANES), F32),
                        pltpu.VMEM(((CONV_WIDTH - 1) * SUBLANES + window, CONV_CHANNELS), F32),
                        pltpu.VMEM((CONV_CHANNELS // LANES, window, LANES), F32)],
        compiler_params=pltpu.CompilerParams(
            dimension_semantics=("parallel",), vmem_limit_bytes=VMEM_LIMIT_BYTES),
        name="in_proj",
    )(x2, x2, w_in_b, cos_t, sin_t, conv_w, conv_b, conv_ln_g, conv_ln_b)


def _attn_kernel(q_ref, kc_ref, kp_ref, vc_ref, vp_ref, o_ref, lse_ref, kbuf, vbuf, bias_ref):
    tq = q_ref.shape[1]
    tile = pl.program_id(2)
    kbuf[0:Q_BLOCK, :] = kp_ref[0]
    kbuf[Q_BLOCK:Q_BLOCK + tq, :] = kc_ref[0]
    vbuf[0:Q_BLOCK, :] = vp_ref[0]
    vbuf[Q_BLOCK:Q_BLOCK + tq, :] = vc_ref[0]

    rows2 = HEADS_PER_VREG * Q_BLOCK
    keys = 2 * Q_BLOCK
    qi = lax.broadcasted_iota(jnp.int32, (rows2, keys), 0) & (Q_BLOCK - 1)
    kk = lax.broadcasted_iota(jnp.int32, (rows2, keys), 1)
    dist = qi + Q_BLOCK - kk
    in_window = (dist >= 0) & (dist <= Q_BLOCK)
    bias_ref[0] = jnp.where(in_window, 0.0, MASK_VALUE)
    bias_ref[1] = jnp.where(in_window & (kk >= Q_BLOCK), 0.0, MASK_VALUE)

    lane = lax.broadcasted_iota(jnp.int32, (Q_BLOCK, LANES), 1)
    head_a = lane < HEAD_DIM
    lse_group = lane // LSE_LANES_PER_HEAD
    ones_cols = jnp.ones((keys, LANES), BF16)

    def sub_block(j, carry):
        r0 = pl.multiple_of(j * Q_BLOCK, Q_BLOCK)
        at_start = jnp.logical_and(tile == 0, j == 0).astype(jnp.int32)
        bias = bias_ref[at_start]
        lse_tile = jnp.zeros((Q_BLOCK, LANES), F32)
        for hp in range(N_HEAD_PAIRS):
            cols = slice(hp * LANES, (hp + 1) * LANES)
            q2 = q_ref[0, pl.ds(r0, Q_BLOCK), cols]
            zero = jnp.zeros_like(q2)
            qs = jnp.concatenate([jnp.where(head_a, q2, zero), jnp.where(head_a, zero, q2)], axis=0)
            k2 = kbuf[pl.ds(r0, keys), cols]
            v2 = vbuf[pl.ds(r0, keys), cols]
            s = lax.dot_general(qs, k2, (((1,), (1,)), ((), ())), preferred_element_type=F32) + bias
            m = jnp.max(s, axis=1, keepdims=True)
            p = jnp.exp(s - m).astype(BF16)
            ov = jnp.dot(p, jnp.concatenate([v2, ones_cols], axis=1), preferred_element_type=F32)
            den = ov[:, LANES:]
            o = ov[:, :LANES] * (1.0 / den)
            o_ref[0, pl.ds(r0, Q_BLOCK), cols] = jnp.where(head_a, o[:Q_BLOCK], o[Q_BLOCK:]).astype(BF16)
            lse = m + jnp.log(den)
            lse_tile = jnp.where(lse_group == HEADS_PER_VREG * hp, lse[:Q_BLOCK], lse_tile)
            lse_tile = jnp.where(lse_group == HEADS_PER_VREG * hp + 1, lse[Q_BLOCK:], lse_tile)
        lse_ref[0, pl.ds(r0, Q_BLOCK), :] = lse_tile
        return carry

    lax.fori_loop(0, tq // Q_BLOCK, sub_block, 0)


def _attention_group(q3, k3, v3, dilation, tq):
    batch, sub_len, _ = q3.shape
    tq = min(tq, sub_len)
    prev_blocks_per_tile = tq // Q_BLOCK
    cur = lambda b, r, n: (b, n, r)
    prev = lambda b, r, n: (b, jnp.maximum(n * prev_blocks_per_tile - 1, 0), r)
    return pl.pallas_call(
        _attn_kernel,
        out_shape=[jax.ShapeDtypeStruct(q3.shape, BF16),
                   jax.ShapeDtypeStruct((batch, sub_len, dilation * LANES), F32)],
        grid=(batch, dilation, sub_len // tq),
        in_specs=[
            pl.BlockSpec((1, tq, ATTN_WIDTH), cur),
            pl.BlockSpec((1, tq, ATTN_WIDTH), cur),
            pl.BlockSpec((1, Q_BLOCK, ATTN_WIDTH), prev),
            pl.BlockSpec((1, tq, ATTN_WIDTH), cur),
            pl.BlockSpec((1, Q_BLOCK, ATTN_WIDTH), prev),
        ],
        out_specs=[pl.BlockSpec((1, tq, ATTN_WIDTH), cur),
                   pl.BlockSpec((1, tq, LANES), cur)],
        scratch_shapes=[
            pltpu.VMEM((Q_BLOCK + tq, ATTN_WIDTH), BF16),
            pltpu.VMEM((Q_BLOCK + tq, ATTN_WIDTH), BF16),
            pltpu.VMEM((2, HEADS_PER_VREG * Q_BLOCK, 2 * Q_BLOCK), F32),
        ],
        compiler_params=pltpu.CompilerParams(
            dimension_semantics=("parallel", "parallel", "arbitrary"),
            vmem_limit_bytes=VMEM_LIMIT_BYTES),
        name=f"attn_d{dilation}",
    )(q3, k3, k3, v3, v3)


def _mix_out_kernel(o1_ref, o2_ref, o3_ref, l1_ref, l2_ref, l3_ref, c_ref, ga_ref, gc_ref, x_ref,
                    wa_ref, wc_ref, wo_ref, g_ref, b_ref, out_ref, o_nat_ref, l_nat_ref, *, alpha):
    tm = x_ref.shape[0]
    n_slabs = ATTN_WIDTH // LANES

    def token_major(g, o_ref, l_ref, d):
        if d == 1:
            return o_ref[0].astype(F32), l_ref[0]
        for r in range(d):
            rows = pl.ds(r, tm // d, stride=d)
            piece = o_ref[0, :, r * ATTN_WIDTH:(r + 1) * ATTN_WIDTH].astype(F32)
            for c in range(n_slabs):
                o_nat_ref[g, c, rows, :] = piece[:, c * LANES:(c + 1) * LANES]
            l_nat_ref[g, rows, :] = l_ref[0, :, r * LANES:(r + 1) * LANES]
        return jnp.concatenate([o_nat_ref[g, c] for c in range(n_slabs)], axis=1), l_nat_ref[g]

    os_, lses = [], []
    for g, (o_ref, l_ref) in enumerate(((o1_ref, l1_ref), (o2_ref, l2_ref), (o3_ref, l3_ref))):
        o, l = token_major(g, o_ref, l_ref, DILATED_GROUPS[g][1])
        os_.append(o)
        lses.append(l)
    m = jnp.maximum(jnp.maximum(lses[0], lses[1]), lses[2])
    es = [jnp.exp(l - m) for l in lses]
    inv = 1.0 / (es[0] + es[1] + es[2])
    src = lax.broadcasted_iota(jnp.int32, (LANES, ATTN_WIDTH), 0)
    dst_head = lax.broadcasted_iota(jnp.int32, (LANES, ATTN_WIDTH), 1) // HEAD_DIM
    spread = jnp.where(src == dst_head * LSE_LANES_PER_HEAD, 1.0, 0.0).astype(BF16)
    o_mix = jnp.zeros((tm, ATTN_WIDTH), F32)
    for e, o in zip(es, os_):
        w = e * inv
        w_hi = w.astype(BF16)
        w_lo = (w - w_hi.astype(F32)).astype(BF16)
        w_full = (jnp.dot(w_hi, spread, preferred_element_type=F32)
                  + jnp.dot(w_lo, spread, preferred_element_type=F32))
        o_mix = o_mix + w_full * o
    y_attn = jnp.dot(o_mix.astype(BF16), wa_ref[...], preferred_element_type=F32)
    y_conv = jnp.dot(c_ref[...], wc_ref[...], preferred_element_type=F32)
    merged = ga_ref[...].astype(F32) * y_attn + gc_ref[...].astype(F32) * y_conv
    mix = jnp.dot(merged.astype(BF16), wo_ref[...], preferred_element_type=F32)
    out_ref[...] = _layer_norm_rows(alpha * x_ref[...] + mix, g_ref[...], b_ref[...])


def _mix_out(os_, lses, c, ga, gc, x2, wa, wc, wo, ln_g, ln_b, alpha, seq, tm):
    tokens = x2.shape[0]
    n_seq_tiles = seq // tm
    row = lambda i: (i, 0)
    view = lambda i: (i // n_seq_tiles, i % n_seq_tiles, 0)
    const = lambda i: (0, 0)
    resident = lambda a: pl.BlockSpec(a.shape, const, pipeline_mode=pl.Buffered(1))
    o_specs = [pl.BlockSpec((1, tm // d, d * ATTN_WIDTH), view) for _, d in DILATED_GROUPS]
    l_specs = [pl.BlockSpec((1, tm // d, d * LANES), view) for _, d in DILATED_GROUPS]
    return pl.pallas_call(
        functools.partial(_mix_out_kernel, alpha=alpha),
        out_shape=jax.ShapeDtypeStruct((tokens, D_MODEL), F32),
        grid=(tokens // tm,),
        in_specs=(o_specs + l_specs
                  + [pl.BlockSpec((tm, CONV_CHANNELS), row)]
                  + [pl.BlockSpec((tm, D_MODEL), row)] * 3
                  + [resident(wa), resident(wc), resident(wo), resident(ln_g), resident(ln_b)]),
        out_specs=pl.BlockSpec((tm, D_MODEL), row),
        scratch_shapes=[pltpu.VMEM((N_GROUPS, ATTN_WIDTH // LANES, tm, LANES), F32),
                        pltpu.VMEM((N_GROUPS, tm, LANES), F32)],
        compiler_params=pltpu.CompilerParams(
            dimension_semantics=("parallel",), vmem_limit_bytes=VMEM_LIMIT_BYTES),
        name="mix_out",
    )(*os_, *lses, c, ga, gc, x2, wa, wc, wo, ln_g, ln_b)


def _ffn_kernel(x_ref, wg_ref, wu_ref, wd_ref, g_ref, b_ref, out_ref, h_ref, *, alpha, chunk):
    x = x_ref[...]
    xb = x.astype(BF16)
    d_ff = wg_ref.shape[1]
    for c0 in range(0, d_ff, chunk):
        gate = jnp.dot(xb, wg_ref[:, c0:c0 + chunk], preferred_element_type=F32)
        up = jnp.dot(xb, wu_ref[:, c0:c0 + chunk], preferred_element_type=F32)
        h_ref[:, c0:c0 + chunk] = (gate * jax.nn.sigmoid(gate) * up).astype(BF16)
    ff = jnp.dot(h_ref[...], wd_ref[...], preferred_element_type=F32)
    out_ref[...] = _layer_norm_rows(alpha * x + ff, g_ref[...], b_ref[...])


def _ffn(x1, wg, wu, wd, ln_g, ln_b, alpha, tm):
    tokens = x1.shape[0]
    d_ff = wg.shape[1]
    row = lambda i: (i, 0)
    const = lambda i: (0, 0)
    resident = lambda a: pl.BlockSpec(a.shape, const, pipeline_mode=pl.Buffered(1))
    return pl.pallas_call(
        functools.partial(_ffn_kernel, alpha=alpha, chunk=256),
        out_shape=jax.ShapeDtypeStruct((tokens, D_MODEL), F32),
        grid=(tokens // tm,),
        in_specs=[pl.BlockSpec((tm, D_MODEL), row),
                  resident(wg), resident(wu), resident(wd), resident(ln_g), resident(ln_b)],
        out_specs=pl.BlockSpec((tm, D_MODEL), row),
        scratch_shapes=[pltpu.VMEM((tm, d_ff), BF16)],
        compiler_params=pltpu.CompilerParams(
            dimension_semantics=("parallel",), vmem_limit_bytes=VMEM_LIMIT_BYTES),
        name="ffn",
    )(x1, wg, wu, wd, ln_g, ln_b)


def _rotary_tables(seq):
    inv_freq = 1.0 / (ROPE_THETA ** (jnp.arange(0, HEAD_DIM, 2, dtype=F32) / HEAD_DIM))
    ang = jnp.arange(seq, dtype=F32)[:, None] * inv_freq[None, :]
    cos, sin = lax.optimization_barrier((jnp.cos(ang), jnp.sin(ang)))
    cos_t = jnp.concatenate([cos, cos] * HEADS_PER_VREG, axis=1)
    sin_t = jnp.concatenate([-sin, sin] * HEADS_PER_VREG, axis=1)
    return cos_t, sin_t


def kernel(x, w_in, conv_w, conv_b, conv_ln_g, conv_ln_b, w_attn_out, w_conv_out, w_o, ln1_g, ln1_b,
           w_ffn_gate, w_ffn_up, w_ffn_down, ln2_g, ln2_b):
    batch, seq, d_model = x.shape
    depth = w_in.shape[0]
    alpha = (2 * depth) ** 0.25
    tokens = batch * seq
    tm = 512
    cos_t, sin_t = _rotary_tables(seq)
    vec = lambda a: a.reshape(1, -1)
    h = x.reshape(tokens, d_model)
    for l in range(depth):
        outs = _in_proj(h, w_in[l].astype(BF16), cos_t, sin_t, conv_w[l], vec(conv_b[l]),
                        vec(conv_ln_g[l]), vec(conv_ln_b[l]), batch, seq, tm)
        qkv, (c, ga, gc) = outs[:3 * N_GROUPS], outs[3 * N_GROUPS:]
        os_, lses = [], []
        for g, (_, dilation) in enumerate(DILATED_GROUPS):
            o, lse = _attention_group(qkv[3 * g], qkv[3 * g + 1], qkv[3 * g + 2], dilation, tq=512)
            os_.append(o)
            lses.append(lse)
        h1 = _mix_out(os_, lses, c, ga, gc, h, w_attn_out[l].astype(BF16), w_conv_out[l].astype(BF16),
                      w_o[l].astype(BF16), vec(ln1_g[l]), vec(ln1_b[l]), alpha, seq, tm)
        h = _ffn(h1, w_ffn_gate[l].astype(BF16), w_ffn_up[l].astype(BF16), w_ffn_down[l].astype(BF16),
                 vec(ln2_g[l]), vec(ln2_b[l]), alpha, tm)
    return h.reshape(batch, seq, d_model)
```

```python
import functools

import jax
import jax.numpy as jnp
from jax import lax
from jax.experimental import pallas as pl
from jax.experimental.pallas import tpu as pltpu

D_MODEL = 1024
HEAD_DIM = 64
N_HEADS = 8
ATTN_WIDTH = N_HEADS * HEAD_DIM
DILATED_GROUPS = ((128, 1), (512, 4), (2048, 16))
N_GROUPS = len(DILATED_GROUPS)
CONV_CHANNELS = 512
CONV_WIDTH = 31
ROPE_THETA = 10000.0
LN_EPS = 1e-5
QKV_COLS = N_GROUPS * 3 * ATTN_WIDTH
CONV_COL0 = QKV_COLS
GATE_COL0 = QKV_COLS + 2 * CONV_CHANNELS

LANES = 128
SUBLANES = 8
HEADS_PER_VREG = LANES // HEAD_DIM
N_HEAD_PAIRS = N_HEADS // HEADS_PER_VREG
STAT_LANES_PER_HEAD = LANES // N_HEADS
LOG2_E = 1.4426950408889634
Q_BLOCK = 128
CONV_HALO = 32
MASK_VALUE = -1e30
VMEM_LIMIT_BYTES = 56 * 1024 * 1024

F32 = jnp.float32
BF16 = jnp.bfloat16

assert all(window // dilation == Q_BLOCK for window, dilation in DILATED_GROUPS)


def _layer_norm_rows(h, g, b):
    mu = jnp.mean(h, axis=-1, keepdims=True)
    d = h - mu
    var = jnp.mean(d * d, axis=-1, keepdims=True)
    return d * lax.rsqrt(var + LN_EPS) * g + b


def _in_proj_kernel(x_ref, xh_ref, w_ref, cos_ref, sin_ref, cw_ref, cb_ref, cg_ref, cbeta_ref, *rest,
                    n_seq_tiles):
    out_refs, (slab_ref, ext_ref, cnat_ref) = rest[:-3], rest[-3:]
    qkv_refs = out_refs[:3 * N_GROUPS]
    c_ref, ga_ref, gc_ref = out_refs[3 * N_GROUPS:]
    tm = x_ref.shape[0]
    n_slabs = x_ref.shape[1] // LANES
    xb = x_ref[...].astype(BF16)
    first_tile = (pl.program_id(0) % n_seq_tiles) == 0
    halo = jnp.where(first_tile, 0.0, xh_ref[...])
    for c in range(n_slabs):
        slab_ref[c, 0:CONV_HALO, :] = halo[:, c * LANES:(c + 1) * LANES]
        slab_ref[c, CONV_HALO:CONV_HALO + tm, :] = x_ref[:, c * LANES:(c + 1) * LANES]
    lane = lax.broadcasted_iota(jnp.int32, (tm, LANES), 1)
    first_half = (lane & (HEAD_DIM // 2)) == 0

    def residue_major(d):
        if d == 1:
            return xb
        rows = [jnp.concatenate([slab_ref[c, pl.ds(CONV_HALO + r, tm // d, stride=d), :]
                                 for c in range(n_slabs)], axis=1) for r in range(d)]
        return jnp.concatenate(rows, axis=0).astype(BF16)

    def table_rows(ref, d):
        if d == 1:
            return ref[...]
        return jnp.concatenate([ref[pl.ds(r, tm // d, stride=d), :] for r in range(d)], axis=0)

    def mm(lhs, col0, width):
        return jnp.dot(lhs, w_ref[:, col0:col0 + width], preferred_element_type=F32)

    window = CONV_HALO + tm
    pitch = window // SUBLANES
    x_win = jnp.concatenate(
        [jnp.concatenate([slab_ref[c, pl.ds(a, SUBLANES, stride=pitch), :] for c in range(n_slabs)], axis=1)
         for a in range(pitch)], axis=0).astype(BF16)
    u = mm(x_win, CONV_COL0, CONV_CHANNELS) * jax.nn.sigmoid(mm(x_win, CONV_COL0 + CONV_CHANNELS, CONV_CHANNELS))
    n_wrap = CONV_WIDTH - 1
    ext_ref[n_wrap * SUBLANES:, :] = u
    ext_ref[0:n_wrap * SUBLANES, :] = ext_ref[pitch * SUBLANES - 1:(pitch + n_wrap) * SUBLANES - 1, :]
    conv_bias, conv_gain, conv_beta = cb_ref[...], cg_ref[...], cbeta_ref[...]
    vregs_per_chunk = 4
    for a0 in range(0, pitch, vregs_per_chunk):
        accs = [jnp.zeros((SUBLANES, CONV_CHANNELS), F32) for _ in range(vregs_per_chunk)]
        for k in range(CONV_WIDTH):
            tap = cw_ref[(n_wrap - k) * SUBLANES:(n_wrap - k + 1) * SUBLANES, :]
            for i in range(vregs_per_chunk):
                e = n_wrap + a0 + i - k
                accs[i] = accs[i] + ext_ref[e * SUBLANES:(e + 1) * SUBLANES, :] * tap
        y = _layer_norm_rows(jnp.concatenate(accs, axis=0) + conv_bias, conv_gain, conv_beta)
        y = y * jax.nn.sigmoid(y)
        for i in range(vregs_per_chunk):
            for c in range(CONV_CHANNELS // LANES):
                cnat_ref[c, pl.ds(a0 + i, SUBLANES, stride=pitch), :] = (
                    y[i * SUBLANES:(i + 1) * SUBLANES, c * LANES:(c + 1) * LANES])
    c_ref[...] = jnp.concatenate([cnat_ref[c, CONV_HALO:CONV_HALO + tm, :]
                                  for c in range(CONV_CHANNELS // LANES)], axis=1).astype(BF16)

    def rope(h, cos_t, sin_t):
        outs = []
        for c in range(h.shape[1] // LANES):
            hs = h[:, c * LANES:(c + 1) * LANES]
            partner = jnp.where(first_half,
                                pltpu.roll(hs, LANES - HEAD_DIM // 2, 1),
                                pltpu.roll(hs, HEAD_DIM // 2, 1))
            outs.append(hs * cos_t + partner * sin_t)
        return jnp.concatenate(outs, axis=1)

    def store_residue_view(ref, h, d):
        rows = tm // d
        for r in range(d):
            ref[0, :, r * ATTN_WIDTH:(r + 1) * ATTN_WIDTH] = h[r * rows:(r + 1) * rows, :].astype(BF16)

    for g, (_, d) in enumerate(DILATED_GROUPS):
        base = g * 3 * ATTN_WIDTH
        xg = residue_major(d)
        cos_t = table_rows(cos_ref, d)
        sin_t = table_rows(sin_ref, d)
        q = rope(mm(xg, base, ATTN_WIDTH), cos_t, sin_t) * (LOG2_E * HEAD_DIM ** -0.5)
        store_residue_view(qkv_refs[3 * g], q, d)
        store_residue_view(qkv_refs[3 * g + 1], rope(mm(xg, base + ATTN_WIDTH, ATTN_WIDTH), cos_t, sin_t), d)
        store_residue_view(qkv_refs[3 * g + 2], mm(xg, base + 2 * ATTN_WIDTH, ATTN_WIDTH), d)

    ga_ref[...] = jax.nn.sigmoid(mm(xb, GATE_COL0, D_MODEL)).astype(BF16)
    gc_ref[...] = jax.nn.sigmoid(mm(xb, GATE_COL0 + D_MODEL, D_MODEL)).astype(BF16)


def _in_proj(x2, w_in_b, cos_t, sin_t, conv_taps, conv_b, conv_ln_g, conv_ln_b, batch, seq, tm):
    tokens = x2.shape[0]
    n_seq_tiles = seq // tm
    halo_blocks_per_tile = tm // CONV_HALO
    window = CONV_HALO + tm
    assert window % SUBLANES == 0 and (window // SUBLANES) % 8 != 0
    row = lambda i: (i, 0)
    const = lambda i: (0, 0)
    tab = lambda i: (i % n_seq_tiles, 0)
    view = lambda i: (i // n_seq_tiles, i % n_seq_tiles, 0)
    vec = pl.BlockSpec((1, CONV_CHANNELS), const)
    qkv_shapes, qkv_specs = [], []
    for _, d in DILATED_GROUPS:
        qkv_shapes += [jax.ShapeDtypeStruct((batch, seq // d, d * ATTN_WIDTH), BF16)] * 3
        qkv_specs += [pl.BlockSpec((1, tm // d, d * ATTN_WIDTH), view)] * 3
    out_shape = (qkv_shapes
                 + [jax.ShapeDtypeStruct((tokens, CONV_CHANNELS), BF16)]
                 + [jax.ShapeDtypeStruct((tokens, D_MODEL), BF16)] * 2)
    out_specs = (qkv_specs
                 + [pl.BlockSpec((tm, CONV_CHANNELS), row)]
                 + [pl.BlockSpec((tm, D_MODEL), row)] * 2)
    return pl.pallas_call(
        functools.partial(_in_proj_kernel, n_seq_tiles=n_seq_tiles),
        out_shape=out_shape,
        grid=(tokens // tm,),
        in_specs=[
            pl.BlockSpec((tm, D_MODEL), row),
            pl.BlockSpec((CONV_HALO, D_MODEL), lambda i: (jnp.maximum(i * halo_blocks_per_tile - 1, 0), 0)),
            pl.BlockSpec(w_in_b.shape, const, pipeline_mode=pl.Buffered(1)),
            pl.BlockSpec((tm, LANES), tab),
            pl.BlockSpec((tm, LANES), tab),
            pl.BlockSpec((CONV_WIDTH * SUBLANES, CONV_CHANNELS), const),
            vec, vec, vec,
        ],
        out_specs=out_specs,
        scratch_shapes=[pltpu.VMEM((D_MODEL // LANES, window, LANES), F32),
                        pltpu.VMEM(((CONV_WIDTH - 1) * SUBLANES + window, CONV_CHANNELS), F32),
                        pltpu.VMEM((CONV_CHANNELS // LANES, window, LANES), F32)],
        compiler_params=pltpu.CompilerParams(
            dimension_semantics=("parallel",), vmem_limit_bytes=VMEM_LIMIT_BYTES),
        name="in_proj",
    )(x2, x2, w_in_b, cos_t, sin_t, conv_taps, conv_b, conv_ln_g, conv_ln_b)


def _attn_kernel(q_ref, kc_ref, kp_ref, vc_ref, vp_ref, o_ref, stat_ref, kbuf, vbuf, bias_ref):
    tq = q_ref.shape[1]
    tile = pl.program_id(2)
    kbuf[0:Q_BLOCK, :] = kp_ref[0]
    kbuf[Q_BLOCK:Q_BLOCK + tq, :] = kc_ref[0]
    vbuf[0:Q_BLOCK, :] = vp_ref[0]
    vbuf[Q_BLOCK:Q_BLOCK + tq, :] = vc_ref[0]

    rows2 = HEADS_PER_VREG * Q_BLOCK
    keys = 2 * Q_BLOCK
    qi = lax.broadcasted_iota(jnp.int32, (rows2, keys), 0) & (Q_BLOCK - 1)
    kk = lax.broadcasted_iota(jnp.int32, (rows2, keys), 1)
    dist = qi + Q_BLOCK - kk
    in_window = (dist >= 0) & (dist <= Q_BLOCK)
    bias_ref[0] = jnp.where(in_window, 0.0, MASK_VALUE)
    bias_ref[1] = jnp.where(in_window & (kk >= Q_BLOCK), 0.0, MASK_VALUE)

    lane = lax.broadcasted_iota(jnp.int32, (Q_BLOCK, LANES), 1)
    head_a = lane < HEAD_DIM
    stat_head = lane // STAT_LANES_PER_HEAD
    stat_is_max = (lane % STAT_LANES_PER_HEAD) < STAT_LANES_PER_HEAD // 2
    ones_cols = jnp.ones((keys, LANES), BF16)

    for j in range(tq // Q_BLOCK):
        r0 = j * Q_BLOCK
        bias = bias_ref[(tile == 0).astype(jnp.int32)] if j == 0 else bias_ref[0]
        stat_tile = jnp.zeros((Q_BLOCK, LANES), F32)
        for hp in range(N_HEAD_PAIRS):
            cols = slice(hp * LANES, (hp + 1) * LANES)
            q2 = q_ref[0, r0:r0 + Q_BLOCK, cols]
            zero = jnp.zeros_like(q2)
            qs = jnp.concatenate([jnp.where(head_a, q2, zero), jnp.where(head_a, zero, q2)], axis=0)
            k2 = kbuf[r0:r0 + keys, cols]
            v2 = vbuf[r0:r0 + keys, cols]
            s = lax.dot_general(qs, k2, (((1,), (1,)), ((), ())), preferred_element_type=F32) + bias
            m = jnp.max(s, axis=1, keepdims=True)
            p = jnp.exp2(s - m).astype(BF16)
            ov = jnp.dot(p, jnp.concatenate([v2, ones_cols], axis=1), preferred_element_type=F32)
            o_ref[0, r0:r0 + Q_BLOCK, cols] = jnp.where(
                head_a, ov[:Q_BLOCK, :LANES], ov[Q_BLOCK:, :LANES]).astype(BF16)
            m_lanes = jnp.broadcast_to(m, (rows2, LANES))
            for half in range(HEADS_PER_VREG):
                rows = slice(half * Q_BLOCK, (half + 1) * Q_BLOCK)
                stat = jnp.where(stat_is_max, m_lanes[rows], ov[rows, LANES:])
                stat_tile = jnp.where(stat_head == HEADS_PER_VREG * hp + half, stat, stat_tile)
        stat_ref[0, r0:r0 + Q_BLOCK, :] = stat_tile


def _attention_group(q3, k3, v3, dilation, tq):
    batch, sub_len, _ = q3.shape
    tq = min(tq, sub_len)
    prev_blocks_per_tile = tq // Q_BLOCK
    cur = lambda b, r, n: (b, n, r)
    prev = lambda b, r, n: (b, jnp.maximum(n * prev_blocks_per_tile - 1, 0), r)
    return pl.pallas_call(
        _attn_kernel,
        out_shape=[jax.ShapeDtypeStruct(q3.shape, BF16),
                   jax.ShapeDtypeStruct((batch, sub_len, dilation * LANES), F32)],
        grid=(batch, dilation, sub_len // tq),
        in_specs=[
            pl.BlockSpec((1, tq, ATTN_WIDTH), cur),
            pl.BlockSpec((1, tq, ATTN_WIDTH), cur),
            pl.BlockSpec((1, Q_BLOCK, ATTN_WIDTH), prev),
            pl.BlockSpec((1, tq, ATTN_WIDTH), cur),
            pl.BlockSpec((1, Q_BLOCK, ATTN_WIDTH), prev),
        ],
        out_specs=[pl.BlockSpec((1, tq, ATTN_WIDTH), cur),
                   pl.BlockSpec((1, tq, LANES), cur)],
        scratch_shapes=[
            pltpu.VMEM((Q_BLOCK + tq, ATTN_WIDTH), BF16),
            pltpu.VMEM((Q_BLOCK + tq, ATTN_WIDTH), BF16),
            pltpu.VMEM((2, HEADS_PER_VREG * Q_BLOCK, 2 * Q_BLOCK), F32),
        ],
        compiler_params=pltpu.CompilerParams(
            dimension_semantics=("parallel", "parallel", "arbitrary"),
            vmem_limit_bytes=VMEM_LIMIT_BYTES),
        name=f"attn_d{dilation}",
    )(q3, k3, k3, v3, v3)


def _mix_out_kernel(o1_ref, o2_ref, o3_ref, l1_ref, l2_ref, l3_ref, c_ref, ga_ref, gc_ref, x_ref,
                    wa_ref, wc_ref, wo_ref, g_ref, b_ref, out_ref, o_nat_ref, l_nat_ref, *, alpha):
    tm = x_ref.shape[0]
    n_slabs = ATTN_WIDTH // LANES

    def token_major(g, o_ref, l_ref, d):
        if d == 1:
            return o_ref[0].astype(F32), l_ref[0]
        for r in range(d):
            rows = pl.ds(r, tm // d, stride=d)
            piece = o_ref[0, :, r * ATTN_WIDTH:(r + 1) * ATTN_WIDTH].astype(F32)
            for c in range(n_slabs):
                o_nat_ref[g, c, rows, :] = piece[:, c * LANES:(c + 1) * LANES]
            l_nat_ref[g, rows, :] = l_ref[0, :, r * LANES:(r + 1) * LANES]
        return jnp.concatenate([o_nat_ref[g, c] for c in range(n_slabs)], axis=1), l_nat_ref[g]

    os_, stats = [], []
    for g, (o_ref, l_ref) in enumerate(((o1_ref, l1_ref), (o2_ref, l2_ref), (o3_ref, l3_ref))):
        o, stat = token_major(g, o_ref, l_ref, DILATED_GROUPS[g][1])
        os_.append(o)
        stats.append(stat)
    lane = lax.broadcasted_iota(jnp.int32, (tm, LANES), 1)
    is_max_lane = (lane % STAT_LANES_PER_HEAD) < STAT_LANES_PER_HEAD // 2
    m = jnp.maximum(jnp.maximum(stats[0], stats[1]), stats[2])
    es = [jnp.exp2(stat - m) for stat in stats]
    dens = [pltpu.roll(stat, LANES - STAT_LANES_PER_HEAD // 2, 1) for stat in stats]
    inv = 1.0 / (es[0] * dens[0] + es[1] * dens[1] + es[2] * dens[2])
    src = lax.broadcasted_iota(jnp.int32, (LANES, ATTN_WIDTH), 0)
    dst_head = lax.broadcasted_iota(jnp.int32, (LANES, ATTN_WIDTH), 1) // HEAD_DIM
    spread = jnp.where(src == dst_head * STAT_LANES_PER_HEAD, 1.0, 0.0).astype(BF16)
    o_mix = jnp.zeros((tm, ATTN_WIDTH), F32)
    for e, o in zip(es, os_):
        w = jnp.where(is_max_lane, e * inv, 0.0)
        w_hi = w.astype(BF16)
        w_lo = (w - w_hi.astype(F32)).astype(BF16)
        w_full = (jnp.dot(w_hi, spread, preferred_element_type=F32)
                  + jnp.dot(w_lo, spread, preferred_element_type=F32))
        o_mix = o_mix + w_full * o
    y_attn = jnp.dot(o_mix.astype(BF16), wa_ref[...], preferred_element_type=F32)
    y_conv = jnp.dot(c_ref[...], wc_ref[...], preferred_element_type=F32)
    merged = ga_ref[...].astype(F32) * y_attn + gc_ref[...].astype(F32) * y_conv
    mix = jnp.dot(merged.astype(BF16), wo_ref[...], preferred_element_type=F32)
    out_ref[...] = _layer_norm_rows(alpha * x_ref[...] + mix, g_ref[...], b_ref[...])


def _mix_out(os_, stats, c, ga, gc, x2, wa, wc, wo, ln_g, ln_b, alpha, seq, tm):
    tokens = x2.shape[0]
    n_seq_tiles = seq // tm
    row = lambda i: (i, 0)
    view = lambda i: (i // n_seq_tiles, i % n_seq_tiles, 0)
    const = lambda i: (0, 0)
    resident = lambda a: pl.BlockSpec(a.shape, const, pipeline_mode=pl.Buffered(1))
    o_specs = [pl.BlockSpec((1, tm // d, d * ATTN_WIDTH), view) for _, d in DILATED_GROUPS]
    l_specs = [pl.BlockSpec((1, tm // d, d * LANES), view) for _, d in DILATED_GROUPS]
    return pl.pallas_call(
        functools.partial(_mix_out_kernel, alpha=alpha),
        out_shape=jax.ShapeDtypeStruct((tokens, D_MODEL), F32),
        grid=(tokens // tm,),
        in_specs=(o_specs + l_specs
                  + [pl.BlockSpec((tm, CONV_CHANNELS), row)]
                  + [pl.BlockSpec((tm, D_MODEL), row)] * 3
                  + [resident(wa), resident(wc), resident(wo), resident(ln_g), resident(ln_b)]),
        out_specs=pl.BlockSpec((tm, D_MODEL), row),
        scratch_shapes=[pltpu.VMEM((N_GROUPS, ATTN_WIDTH // LANES, tm, LANES), F32),
                        pltpu.VMEM((N_GROUPS, tm, LANES), F32)],
        compiler_params=pltpu.CompilerParams(
            dimension_semantics=("parallel",), vmem_limit_bytes=VMEM_LIMIT_BYTES),
        name="mix_out",
    )(*os_, *stats, c, ga, gc, x2, wa, wc, wo, ln_g, ln_b)


def _ffn_kernel(x_ref, wg_ref, wu_ref, wd_ref, g_ref, b_ref, out_ref, h_ref, *, alpha, chunk):
    x = x_ref[...]
    xb = x.astype(BF16)
    d_ff = wg_ref.shape[1]
    for c0 in range(0, d_ff, chunk):
        gate = jnp.dot(xb, wg_ref[:, c0:c0 + chunk], preferred_element_type=F32)
        up = jnp.dot(xb, wu_ref[:, c0:c0 + chunk], preferred_element_type=F32)
        h_ref[:, c0:c0 + chunk] = (gate * jax.nn.sigmoid(gate) * up).astype(BF16)
    ff = jnp.dot(h_ref[...], wd_ref[...], preferred_element_type=F32)
    out_ref[...] = _layer_norm_rows(alpha * x + ff, g_ref[...], b_ref[...])


def _ffn(x1, wg, wu, wd, ln_g, ln_b, alpha, tm):
    tokens = x1.shape[0]
    d_ff = wg.shape[1]
    row = lambda i: (i, 0)
    const = lambda i: (0, 0)
    resident = lambda a: pl.BlockSpec(a.shape, const, pipeline_mode=pl.Buffered(1))
    return pl.pallas_call(
        functools.partial(_ffn_kernel, alpha=alpha, chunk=256),
        out_shape=jax.ShapeDtypeStruct((tokens, D_MODEL), F32),
        grid=(tokens // tm,),
        in_specs=[pl.BlockSpec((tm, D_MODEL), row),
                  resident(wg), resident(wu), resident(wd), resident(ln_g), resident(ln_b)],
        out_specs=pl.BlockSpec((tm, D_MODEL), row),
        scratch_shapes=[pltpu.VMEM((tm, d_ff), BF16)],
        compiler_params=pltpu.CompilerParams(
            dimension_semantics=("parallel",), vmem_limit_bytes=VMEM_LIMIT_BYTES),
        name="ffn",
    )(x1, wg, wu, wd, ln_g, ln_b)


def _rotary_tables(seq):
    inv_freq = 1.0 / (ROPE_THETA ** (jnp.arange(0, HEAD_DIM, 2, dtype=F32) / HEAD_DIM))
    ang = jnp.arange(seq, dtype=F32)[:, None] * inv_freq[None, :]
    cos, sin = lax.optimization_barrier((jnp.cos(ang), jnp.sin(ang)))
    cos_t = jnp.concatenate([cos, cos] * HEADS_PER_VREG, axis=1)
    sin_t = jnp.concatenate([-sin, sin] * HEADS_PER_VREG, axis=1)
    return cos_t, sin_t


def kernel(x, w_in, conv_w, conv_b, conv_ln_g, conv_ln_b, w_attn_out, w_conv_out, w_o, ln1_g, ln1_b,
           w_ffn_gate, w_ffn_up, w_ffn_down, ln2_g, ln2_b):
    batch, seq, d_model = x.shape
    depth = w_in.shape[0]
    alpha = (2 * depth) ** 0.25
    tokens = batch * seq
    tm = 512
    cos_t, sin_t = _rotary_tables(seq)
    vec = lambda a: a.reshape(1, -1)
    h = x.reshape(tokens, d_model)
    for l in range(depth):
        conv_taps = jnp.repeat(conv_w[l], SUBLANES, axis=0)
        outs = _in_proj(h, w_in[l].astype(BF16), cos_t, sin_t, conv_taps, vec(conv_b[l]),
                        vec(conv_ln_g[l]), vec(conv_ln_b[l]), batch, seq, tm)
        qkv, (c, ga, gc) = outs[:3 * N_GROUPS], outs[3 * N_GROUPS:]
        os_, stats = [], []
        for g, (_, dilation) in enumerate(DILATED_GROUPS):
            o, stat = _attention_group(qkv[3 * g], qkv[3 * g + 1], qkv[3 * g + 2], dilation, tq=512)
            os_.append(o)
            stats.append(stat)
        h1 = _mix_out(os_, stats, c, ga, gc, h, w_attn_out[l].astype(BF16), w_conv_out[l].astype(BF16),
                      w_o[l].astype(BF16), vec(ln1_g[l]), vec(ln1_b[l]), alpha, seq, tm)
        h = _ffn(h1, w_ffn_gate[l].astype(BF16), w_ffn_up[l].astype(BF16), w_ffn_down[l].astype(BF16),
                 vec(ln2_g[l]), vec(ln2_b[l]), alpha, tm)
    return h.reshape(batch, seq, d_model)
```

```python
import functools

import jax
import jax.numpy as jnp
from jax import lax
from jax.experimental import pallas as pl
from jax.experimental.pallas import tpu as pltpu

D_MODEL = 1024
HEAD_DIM = 64
N_HEADS = 8
ATTN_WIDTH = N_HEADS * HEAD_DIM
DILATED_GROUPS = ((128, 1), (512, 4), (2048, 16))
N_GROUPS = len(DILATED_GROUPS)
CONV_CHANNELS = 512
CONV_WIDTH = 31
ROPE_THETA = 10000.0
LN_EPS = 1e-5
QKV_COLS = N_GROUPS * 3 * ATTN_WIDTH
CONV_COL0 = QKV_COLS
GATE_COL0 = QKV_COLS + 2 * CONV_CHANNELS

LANES = 128
SUBLANES = 8
HEADS_PER_VREG = LANES // HEAD_DIM
N_HEAD_PAIRS = N_HEADS // HEADS_PER_VREG
STAT_LANES_PER_HEAD = LANES // N_HEADS
LOG2_E = 1.4426950408889634
Q_BLOCK = 128
CONV_HALO = 32
MASK_VALUE = -1e30
VMEM_LIMIT_BYTES = 56 * 1024 * 1024

F32 = jnp.float32
BF16 = jnp.bfloat16

assert all(window // dilation == Q_BLOCK for window, dilation in DILATED_GROUPS)


def _layer_norm_rows(h, g, b):
    mu = jnp.mean(h, axis=-1, keepdims=True)
    d = h - mu
    var = jnp.mean(d * d, axis=-1, keepdims=True)
    return d * lax.rsqrt(var + LN_EPS) * g + b


def _in_proj_kernel(x_ref, xh_ref, w_ref, cos_ref, sin_ref, cw_ref, cb_ref, cg_ref, cbeta_ref, *rest,
                    n_seq_tiles):
    out_refs, (slab_ref, ext_ref, cnat_ref) = rest[:-3], rest[-3:]
    qkv_refs = out_refs[:3 * N_GROUPS]
    c_ref, ga_ref, gc_ref = out_refs[3 * N_GROUPS:]
    tm = x_ref.shape[0]
    n_slabs = x_ref.shape[1] // LANES
    xb = x_ref[...].astype(BF16)
    first_tile = (pl.program_id(0) % n_seq_tiles) == 0
    halo = jnp.where(first_tile, 0.0, xh_ref[...])
    for c in range(n_slabs):
        slab_ref[c, 0:CONV_HALO, :] = halo[:, c * LANES:(c + 1) * LANES]
        slab_ref[c, CONV_HALO:CONV_HALO + tm, :] = x_ref[:, c * LANES:(c + 1) * LANES]
    lane = lax.broadcasted_iota(jnp.int32, (tm, LANES), 1)
    first_half = (lane & (HEAD_DIM // 2)) == 0

    def residue_major(d):
        if d == 1:
            return xb
        rows = [jnp.concatenate([slab_ref[c, pl.ds(CONV_HALO + r, tm // d, stride=d), :]
                                 for c in range(n_slabs)], axis=1) for r in range(d)]
        return jnp.concatenate(rows, axis=0).astype(BF16)

    def table_rows(ref, d):
        if d == 1:
            return ref[...]
        return jnp.concatenate([ref[pl.ds(r, tm // d, stride=d), :] for r in range(d)], axis=0)

    def mm(lhs, col0, width):
        return jnp.dot(lhs, w_ref[:, col0:col0 + width], preferred_element_type=F32)

    window = CONV_HALO + tm
    pitch = window // SUBLANES
    x_win = jnp.concatenate(
        [jnp.concatenate([slab_ref[c, pl.ds(a, SUBLANES, stride=pitch), :] for c in range(n_slabs)], axis=1)
         for a in range(pitch)], axis=0).astype(BF16)
    u = mm(x_win, CONV_COL0, CONV_CHANNELS) * jax.nn.sigmoid(mm(x_win, CONV_COL0 + CONV_CHANNELS, CONV_CHANNELS))
    n_wrap = CONV_WIDTH - 1
    ext_ref[n_wrap * SUBLANES:, :] = u
    ext_ref[0:n_wrap * SUBLANES, :] = ext_ref[pitch * SUBLANES - 1:(pitch + n_wrap) * SUBLANES - 1, :]
    conv_bias, conv_gain, conv_beta = cb_ref[...], cg_ref[...], cbeta_ref[...]
    vregs_per_chunk = 4
    for a0 in range(0, pitch, vregs_per_chunk):
        accs = [jnp.zeros((SUBLANES, CONV_CHANNELS), F32) for _ in range(vregs_per_chunk)]
        for k in range(CONV_WIDTH):
            tap = cw_ref[(n_wrap - k) * SUBLANES:(n_wrap - k + 1) * SUBLANES, :]
            for i in range(vregs_per_chunk):
                e = n_wrap + a0 + i - k
                accs[i] = accs[i] + ext_ref[e * SUBLANES:(e + 1) * SUBLANES, :] * tap
        y = _layer_norm_rows(jnp.concatenate(accs, axis=0) + conv_bias, conv_gain, conv_beta)
        y = y * jax.nn.sigmoid(y)
        for i in range(vregs_per_chunk):
            for c in range(CONV_CHANNELS // LANES):
                cnat_ref[c, pl.ds(a0 + i, SUBLANES, stride=pitch), :] = (
                    y[i * SUBLANES:(i + 1) * SUBLANES, c * LANES:(c + 1) * LANES])
    c_ref[...] = jnp.concatenate([cnat_ref[c, CONV_HALO:CONV_HALO + tm, :]
                                  for c in range(CONV_CHANNELS // LANES)], axis=1).astype(BF16)

    def rope(h, cos_t, sin_t):
        outs = []
        for c in range(h.shape[1] // LANES):
            hs = h[:, c * LANES:(c + 1) * LANES]
            partner = jnp.where(first_half,
                                pltpu.roll(hs, LANES - HEAD_DIM // 2, 1),
                                pltpu.roll(hs, HEAD_DIM // 2, 1))
            outs.append(hs * cos_t + partner * sin_t)
        return jnp.concatenate(outs, axis=1)

    def store_residue_view(ref, h, d):
        rows = tm // d
        for r in range(d):
            ref[0, :, r * ATTN_WIDTH:(r + 1) * ATTN_WIDTH] = h[r * rows:(r + 1) * rows, :].astype(BF16)

    for g, (_, d) in enumerate(DILATED_GROUPS):
        base = g * 3 * ATTN_WIDTH
        xg = residue_major(d)
        cos_t = table_rows(cos_ref, d)
        sin_t = table_rows(sin_ref, d)
        q = rope(mm(xg, base, ATTN_WIDTH), cos_t, sin_t) * (LOG2_E * HEAD_DIM ** -0.5)
        store_residue_view(qkv_refs[3 * g], q, d)
        store_residue_view(qkv_refs[3 * g + 1], rope(mm(xg, base + ATTN_WIDTH, ATTN_WIDTH), cos_t, sin_t), d)
        store_residue_view(qkv_refs[3 * g + 2], mm(xg, base + 2 * ATTN_WIDTH, ATTN_WIDTH), d)

    ga_ref[...] = jax.nn.sigmoid(mm(xb, GATE_COL0, D_MODEL)).astype(BF16)
    gc_ref[...] = jax.nn.sigmoid(mm(xb, GATE_COL0 + D_MODEL, D_MODEL)).astype(BF16)


def _in_proj(x2, w_in_b, cos_t, sin_t, conv_taps, conv_b, conv_ln_g, conv_ln_b, batch, seq, tm):
    tokens = x2.shape[0]
    n_seq_tiles = seq // tm
    halo_blocks_per_tile = tm // CONV_HALO
    window = CONV_HALO + tm
    assert window % SUBLANES == 0 and (window // SUBLANES) % 8 != 0
    row = lambda i: (i, 0)
    const = lambda i: (0, 0)
    tab = lambda i: (i % n_seq_tiles, 0)
    view = lambda i: (i // n_seq_tiles, i % n_seq_tiles, 0)
    vec = pl.BlockSpec((1, CONV_CHANNELS), const)
    qkv_shapes, qkv_specs = [], []
    for _, d in DILATED_GROUPS:
        qkv_shapes += [jax.ShapeDtypeStruct((batch, seq // d, d * ATTN_WIDTH), BF16)] * 3
        qkv_specs += [pl.BlockSpec((1, tm // d, d * ATTN_WIDTH), view)] * 3
    out_shape = (qkv_shapes
                 + [jax.ShapeDtypeStruct((tokens, CONV_CHANNELS), BF16)]
                 + [jax.ShapeDtypeStruct((tokens, D_MODEL), BF16)] * 2)
    out_specs = (qkv_specs
                 + [pl.BlockSpec((tm, CONV_CHANNELS), row)]
                 + [pl.BlockSpec((tm, D_MODEL), row)] * 2)
    return pl.pallas_call(
        functools.partial(_in_proj_kernel, n_seq_tiles=n_seq_tiles),
        out_shape=out_shape,
        grid=(tokens // tm,),
        in_specs=[
            pl.BlockSpec((tm, D_MODEL), row),
            pl.BlockSpec((CONV_HALO, D_MODEL), lambda i: (jnp.maximum(i * halo_blocks_per_tile - 1, 0), 0)),
            pl.BlockSpec(w_in_b.shape, const, pipeline_mode=pl.Buffered(1)),
            pl.BlockSpec((tm, LANES), tab),
            pl.BlockSpec((tm, LANES), tab),
            pl.BlockSpec((CONV_WIDTH * SUBLANES, CONV_CHANNELS), const),
            vec, vec, vec,
        ],
        out_specs=out_specs,
        scratch_shapes=[pltpu.VMEM((D_MODEL // LANES, window, LANES), F32),
                        pltpu.VMEM(((CONV_WIDTH - 1) * SUBLANES + window, CONV_CHANNELS), F32),
                        pltpu.VMEM((CONV_CHANNELS // LANES, window, LANES), F32)],
        compiler_params=pltpu.CompilerParams(
            dimension_semantics=("parallel",), vmem_limit_bytes=VMEM_LIMIT_BYTES),
        name="in_proj",
    )(x2, x2, w_in_b, cos_t, sin_t, conv_taps, conv_b, conv_ln_g, conv_ln_b)


def _attn_kernel(q_ref, kc_ref, kp_ref, vc_ref, vp_ref, o_ref, stat_ref, bias_ref):
    tq = q_ref.shape[1]
    tile = pl.program_id(2)

    def band(prev_ref, cur_ref, j, cols):
        if j == 0:
            return jnp.concatenate([prev_ref[0, :, cols], cur_ref[0, 0:Q_BLOCK, cols]], axis=0)
        return cur_ref[0, (j - 1) * Q_BLOCK:(j + 1) * Q_BLOCK, cols]

    rows2 = HEADS_PER_VREG * Q_BLOCK
    keys = 2 * Q_BLOCK
    qi = lax.broadcasted_iota(jnp.int32, (rows2, keys), 0) & (Q_BLOCK - 1)
    kk = lax.broadcasted_iota(jnp.int32, (rows2, keys), 1)
    dist = qi + Q_BLOCK - kk
    in_window = (dist >= 0) & (dist <= Q_BLOCK)
    bias_ref[0] = jnp.where(in_window, 0.0, MASK_VALUE)
    bias_ref[1] = jnp.where(in_window & (kk >= Q_BLOCK), 0.0, MASK_VALUE)

    lane = lax.broadcasted_iota(jnp.int32, (Q_BLOCK, LANES), 1)
    head_a = lane < HEAD_DIM
    stat_head = lane // STAT_LANES_PER_HEAD
    stat_is_max = (lane % STAT_LANES_PER_HEAD) < STAT_LANES_PER_HEAD // 2
    ones_cols = jnp.ones((keys, LANES), BF16)

    for j in range(tq // Q_BLOCK):
        r0 = j * Q_BLOCK
        bias = bias_ref[(tile == 0).astype(jnp.int32)] if j == 0 else bias_ref[0]
        stat_tile = jnp.zeros((Q_BLOCK, LANES), F32)
        for hp in range(N_HEAD_PAIRS):
            cols = slice(hp * LANES, (hp + 1) * LANES)
            q2 = q_ref[0, r0:r0 + Q_BLOCK, cols]
            zero = jnp.zeros_like(q2)
            qs = jnp.concatenate([jnp.where(head_a, q2, zero), jnp.where(head_a, zero, q2)], axis=0)
            k2 = band(kp_ref, kc_ref, j, cols)
            v2 = band(vp_ref, vc_ref, j, cols)
            s = lax.dot_general(qs, k2, (((1,), (1,)), ((), ())), preferred_element_type=F32) + bias
            m = jnp.max(s, axis=1, keepdims=True)
            p = jnp.exp2(s - m).astype(BF16)
            ov = jnp.dot(p, jnp.concatenate([v2, ones_cols], axis=1), preferred_element_type=F32)
            o_ref[0, r0:r0 + Q_BLOCK, cols] = jnp.where(
                head_a, ov[:Q_BLOCK, :LANES], ov[Q_BLOCK:, :LANES]).astype(BF16)
            m_lanes = jnp.broadcast_to(m, (rows2, LANES))
            for half in range(HEADS_PER_VREG):
                rows = slice(half * Q_BLOCK, (half + 1) * Q_BLOCK)
                stat = jnp.where(stat_is_max, m_lanes[rows], ov[rows, LANES:])
                stat_tile = jnp.where(stat_head == HEADS_PER_VREG * hp + half, stat, stat_tile)
        stat_ref[0, r0:r0 + Q_BLOCK, :] = stat_tile


def _attention_group(q3, k3, v3, dilation, tq):
    batch, sub_len, _ = q3.shape
    tq = min(tq, sub_len)
    prev_blocks_per_tile = tq // Q_BLOCK
    cur = lambda b, r, n: (b, n, r)
    prev = lambda b, r, n: (b, jnp.maximum(n * prev_blocks_per_tile - 1, 0), r)
    return pl.pallas_call(
        _attn_kernel,
        out_shape=[jax.ShapeDtypeStruct(q3.shape, BF16),
                   jax.ShapeDtypeStruct((batch, sub_len, dilation * LANES), F32)],
        grid=(batch, dilation, sub_len // tq),
        in_specs=[
            pl.BlockSpec((1, tq, ATTN_WIDTH), cur),
            pl.BlockSpec((1, tq, ATTN_WIDTH), cur),
            pl.BlockSpec((1, Q_BLOCK, ATTN_WIDTH), prev),
            pl.BlockSpec((1, tq, ATTN_WIDTH), cur),
            pl.BlockSpec((1, Q_BLOCK, ATTN_WIDTH), prev),
        ],
        out_specs=[pl.BlockSpec((1, tq, ATTN_WIDTH), cur),
                   pl.BlockSpec((1, tq, LANES), cur)],
        scratch_shapes=[pltpu.VMEM((2, HEADS_PER_VREG * Q_BLOCK, 2 * Q_BLOCK), F32)],
        compiler_params=pltpu.CompilerParams(
            dimension_semantics=("parallel", "parallel", "arbitrary"),
            vmem_limit_bytes=VMEM_LIMIT_BYTES),
        name=f"attn_d{dilation}",
    )(q3, k3, k3, v3, v3)


def _mix_out_kernel(o1_ref, o2_ref, o3_ref, l1_ref, l2_ref, l3_ref, c_ref, ga_ref, gc_ref, x_ref,
                    wa_ref, wc_ref, wo_ref, g_ref, b_ref, out_ref, o_nat_ref, l_nat_ref, *, alpha, sub_rows):
    tm = x_ref.shape[0]
    n_slabs = ATTN_WIDTH // LANES
    for t0 in range(0, tm, sub_rows):
        rows_nat = slice(t0, t0 + sub_rows)

        def token_major(g, o_ref, l_ref, d):
            rows_view = slice(t0 // d, (t0 + sub_rows) // d)
            if d == 1:
                return o_ref[0, rows_view, :].astype(F32), l_ref[0, rows_view, :]
            for r in range(d):
                rows = pl.ds(t0 + r, sub_rows // d, stride=d)
                piece = o_ref[0, rows_view, r * ATTN_WIDTH:(r + 1) * ATTN_WIDTH].astype(F32)
                for c in range(n_slabs):
                    o_nat_ref[g, c, rows, :] = piece[:, c * LANES:(c + 1) * LANES]
                l_nat_ref[g, rows, :] = l_ref[0, rows_view, r * LANES:(r + 1) * LANES]
            return (jnp.concatenate([o_nat_ref[g, c, rows_nat, :] for c in range(n_slabs)], axis=1),
                    l_nat_ref[g, rows_nat, :])

        os_, stats = [], []
        for g, (o_ref, l_ref) in enumerate(((o1_ref, l1_ref), (o2_ref, l2_ref), (o3_ref, l3_ref))):
            o, stat = token_major(g, o_ref, l_ref, DILATED_GROUPS[g][1])
            os_.append(o)
            stats.append(stat)
        lane = lax.broadcasted_iota(jnp.int32, (sub_rows, LANES), 1)
        is_max_lane = (lane % STAT_LANES_PER_HEAD) < STAT_LANES_PER_HEAD // 2
        m = jnp.maximum(jnp.maximum(stats[0], stats[1]), stats[2])
        es = [jnp.exp2(stat - m) for stat in stats]
        dens = [pltpu.roll(stat, LANES - STAT_LANES_PER_HEAD // 2, 1) for stat in stats]
        inv = 1.0 / (es[0] * dens[0] + es[1] * dens[1] + es[2] * dens[2])
        src = lax.broadcasted_iota(jnp.int32, (2 * LANES, ATTN_WIDTH), 0) % LANES
        dst_head = lax.broadcasted_iota(jnp.int32, (2 * LANES, ATTN_WIDTH), 1) // HEAD_DIM
        spread = jnp.where(src == dst_head * STAT_LANES_PER_HEAD, 1.0, 0.0).astype(BF16)
        o_mix = jnp.zeros((sub_rows, ATTN_WIDTH), F32)
        for e, o in zip(es, os_):
            w = jnp.where(is_max_lane, e * inv, 0.0)
            w_hi = w.astype(BF16)
            w_lo = (w - w_hi.astype(F32)).astype(BF16)
            w_full = jnp.dot(jnp.concatenate([w_hi, w_lo], axis=1), spread, preferred_element_type=F32)
            o_mix = o_mix + w_full * o
        y_attn = jnp.dot(o_mix.astype(BF16), wa_ref[...], preferred_element_type=F32)
        y_conv = jnp.dot(c_ref[rows_nat, :], wc_ref[...], preferred_element_type=F32)
        merged = ga_ref[rows_nat, :].astype(F32) * y_attn + gc_ref[rows_nat, :].astype(F32) * y_conv
        mix = jnp.dot(merged.astype(BF16), wo_ref[...], preferred_element_type=F32)
        out_ref[rows_nat, :] = _layer_norm_rows(alpha * x_ref[rows_nat, :] + mix, g_ref[...], b_ref[...])


def _mix_out(os_, stats, c, ga, gc, x2, wa, wc, wo, ln_g, ln_b, alpha, seq, tm):
    tokens = x2.shape[0]
    n_seq_tiles = seq // tm
    row = lambda i: (i, 0)
    view = lambda i: (i // n_seq_tiles, i % n_seq_tiles, 0)
    const = lambda i: (0, 0)
    resident = lambda a: pl.BlockSpec(a.shape, const, pipeline_mode=pl.Buffered(1))
    o_specs = [pl.BlockSpec((1, tm // d, d * ATTN_WIDTH), view) for _, d in DILATED_GROUPS]
    l_specs = [pl.BlockSpec((1, tm // d, d * LANES), view) for _, d in DILATED_GROUPS]
    return pl.pallas_call(
        functools.partial(_mix_out_kernel, alpha=alpha, sub_rows=tm // 2),
        out_shape=jax.ShapeDtypeStruct((tokens, D_MODEL), F32),
        grid=(tokens // tm,),
        in_specs=(o_specs + l_specs
                  + [pl.BlockSpec((tm, CONV_CHANNELS), row)]
                  + [pl.BlockSpec((tm, D_MODEL), row)] * 3
                  + [resident(wa), resident(wc), resident(wo), resident(ln_g), resident(ln_b)]),
        out_specs=pl.BlockSpec((tm, D_MODEL), row),
        scratch_shapes=[pltpu.VMEM((N_GROUPS, ATTN_WIDTH // LANES, tm, LANES), F32),
                        pltpu.VMEM((N_GROUPS, tm, LANES), F32)],
        compiler_params=pltpu.CompilerParams(
            dimension_semantics=("parallel",), vmem_limit_bytes=VMEM_LIMIT_BYTES),
        name="mix_out",
    )(*os_, *stats, c, ga, gc, x2, wa, wc, wo, ln_g, ln_b)


def _ffn_kernel(x_ref, wg_ref, wu_ref, wd_ref, g_ref, b_ref, out_ref, h_ref, *, alpha, chunk):
    x = x_ref[...]
    xb = x.astype(BF16)
    d_ff = wg_ref.shape[1]
    for c0 in range(0, d_ff, chunk):
        gate = jnp.dot(xb, wg_ref[:, c0:c0 + chunk], preferred_element_type=F32)
        up = jnp.dot(xb, wu_ref[:, c0:c0 + chunk], preferred_element_type=F32)
        h_ref[:, c0:c0 + chunk] = (gate * jax.nn.sigmoid(gate) * up).astype(BF16)
    ff = jnp.dot(h_ref[...], wd_ref[...], preferred_element_type=F32)
    out_ref[...] = _layer_norm_rows(alpha * x + ff, g_ref[...], b_ref[...])


def _ffn(x1, wg, wu, wd, ln_g, ln_b, alpha, tm):
    tokens = x1.shape[0]
    d_ff = wg.shape[1]
    row = lambda i: (i, 0)
    const = lambda i: (0, 0)
    resident = lambda a: pl.BlockSpec(a.shape, const, pipeline_mode=pl.Buffered(1))
    return pl.pallas_call(
        functools.partial(_ffn_kernel, alpha=alpha, chunk=256),
        out_shape=jax.ShapeDtypeStruct((tokens, D_MODEL), F32),
        grid=(tokens // tm,),
        in_specs=[pl.BlockSpec((tm, D_MODEL), row),
                  resident(wg), resident(wu), resident(wd), resident(ln_g), resident(ln_b)],
        out_specs=pl.BlockSpec((tm, D_MODEL), row),
        scratch_shapes=[pltpu.VMEM((tm, d_ff), BF16)],
        compiler_params=pltpu.CompilerParams(
            dimension_semantics=("parallel",), vmem_limit_bytes=VMEM_LIMIT_BYTES),
        name="ffn",
    )(x1, wg, wu, wd, ln_g, ln_b)


def _rotary_tables(seq):
    inv_freq = 1.0 / (ROPE_THETA ** (jnp.arange(0, HEAD_DIM, 2, dtype=F32) / HEAD_DIM))
    ang = jnp.arange(seq, dtype=F32)[:, None] * inv_freq[None, :]
    cos, sin = lax.optimization_barrier((jnp.cos(ang), jnp.sin(ang)))
    cos_t = jnp.concatenate([cos, cos] * HEADS_PER_VREG, axis=1)
    sin_t = jnp.concatenate([-sin, sin] * HEADS_PER_VREG, axis=1)
    return cos_t, sin_t


def kernel(x, w_in, conv_w, conv_b, conv_ln_g, conv_ln_b, w_attn_out, w_conv_out, w_o, ln1_g, ln1_b,
           w_ffn_gate, w_ffn_up, w_ffn_down, ln2_g, ln2_b):
    batch, seq, d_model = x.shape
    depth = w_in.shape[0]
    alpha = (2 * depth) ** 0.25
    tokens = batch * seq
    tm = 512
    cos_t, sin_t = _rotary_tables(seq)
    vec = lambda a: a.reshape(1, -1)
    h = x.reshape(tokens, d_model)
    for l in range(depth):
        conv_taps = jnp.repeat(conv_w[l], SUBLANES, axis=0)
        outs = _in_proj(h, w_in[l].astype(BF16), cos_t, sin_t, conv_taps, vec(conv_b[l]),
                        vec(conv_ln_g[l]), vec(conv_ln_b[l]), batch, seq, tm)
        qkv, (c, ga, gc) = outs[:3 * N_GROUPS], outs[3 * N_GROUPS:]
        os_, stats = [], []
        for g, (_, dilation) in enumerate(DILATED_GROUPS):
            o, stat = _attention_group(qkv[3 * g], qkv[3 * g + 1], qkv[3 * g + 2], dilation, tq=512)
            os_.append(o)
            stats.append(stat)
        h1 = _mix_out(os_, stats, c, ga, gc, h, w_attn_out[l].astype(BF16), w_conv_out[l].astype(BF16),
                      w_o[l].astype(BF16), vec(ln1_g[l]), vec(ln1_b[l]), alpha, seq, 2 * tm)
        h = _ffn(h1, w_ffn_gate[l].astype(BF16), w_ffn_up[l].astype(BF16), w_ffn_down[l].astype(BF16),
                 vec(ln2_g[l]), vec(ln2_b[l]), alpha, tm)
    return h.reshape(batch, seq, d_model)
```

```python
import functools

import jax
import jax.numpy as jnp
from jax import lax
from jax.experimental import pallas as pl
from jax.experimental.pallas import tpu as pltpu

D_MODEL = 1024
HEAD_DIM = 64
N_HEADS = 8
ATTN_WIDTH = N_HEADS * HEAD_DIM
DILATED_GROUPS = ((128, 1), (512, 4), (2048, 16))
N_GROUPS = len(DILATED_GROUPS)
CONV_CHANNELS = 512
CONV_WIDTH = 31
ROPE_THETA = 10000.0
LN_EPS = 1e-5
QKV_COLS = N_GROUPS * 3 * ATTN_WIDTH
CONV_COL0 = QKV_COLS
GATE_COL0 = QKV_COLS + 2 * CONV_CHANNELS

LANES = 128
SUBLANES = 8
HEADS_PER_VREG = LANES // HEAD_DIM
N_HEAD_PAIRS = N_HEADS // HEADS_PER_VREG
STAT_LANES_PER_HEAD = LANES // N_HEADS
LOG2_E = 1.4426950408889634
ROPE_SPLIT = 128
Q_BLOCK = 128
CONV_HALO = 32
MASK_VALUE = -1e30
VMEM_LIMIT_BYTES = 56 * 1024 * 1024

F32 = jnp.float32
BF16 = jnp.bfloat16

assert all(window // dilation == Q_BLOCK for window, dilation in DILATED_GROUPS)


def _layer_norm_rows(h, g, b):
    mu = jnp.mean(h, axis=-1, keepdims=True)
    d = h - mu
    var = jnp.mean(d * d, axis=-1, keepdims=True)
    return d * lax.rsqrt(var + LN_EPS) * g + b


def _in_proj_kernel(x_ref, xh_ref, w_ref, cos_ref, sin_ref, cw_ref, cb_ref, cg_ref, cbeta_ref, *rest,
                    n_seq_tiles):
    out_refs, (slab_ref, ext_ref, cnat_ref) = rest[:-3], rest[-3:]
    qkv_refs = out_refs[:3 * N_GROUPS]
    c_ref, ga_ref, gc_ref = out_refs[3 * N_GROUPS:]
    tm = x_ref.shape[0]
    n_slabs = x_ref.shape[1] // LANES
    xb = x_ref[...].astype(BF16)
    first_tile = (pl.program_id(0) % n_seq_tiles) == 0
    halo = jnp.where(first_tile, 0.0, xh_ref[...])
    for c in range(n_slabs):
        slab_ref[c, 0:CONV_HALO, :] = halo[:, c * LANES:(c + 1) * LANES]
        slab_ref[c, CONV_HALO:CONV_HALO + tm, :] = x_ref[:, c * LANES:(c + 1) * LANES]
    lane = lax.broadcasted_iota(jnp.int32, (tm, LANES), 1)
    first_half = (lane & (HEAD_DIM // 2)) == 0

    def residue_major(d):
        if d == 1:
            return xb
        rows = [jnp.concatenate([slab_ref[c, pl.ds(CONV_HALO + r, tm // d, stride=d), :]
                                 for c in range(n_slabs)], axis=1) for r in range(d)]
        return jnp.concatenate(rows, axis=0).astype(BF16)

    def table_rows(ref, d):
        if d == 1:
            return ref[...]
        return jnp.concatenate([ref[pl.ds(r, tm // d, stride=d), :] for r in range(d)], axis=0)

    def mm(lhs, col0, width):
        return jnp.dot(lhs, w_ref[:, col0:col0 + width], preferred_element_type=F32)

    window = CONV_HALO + tm
    pitch = window // SUBLANES
    x_win = jnp.concatenate(
        [jnp.concatenate([slab_ref[c, pl.ds(a, SUBLANES, stride=pitch), :] for c in range(n_slabs)], axis=1)
         for a in range(pitch)], axis=0).astype(BF16)
    u = mm(x_win, CONV_COL0, CONV_CHANNELS) * jax.nn.sigmoid(mm(x_win, CONV_COL0 + CONV_CHANNELS, CONV_CHANNELS))
    n_wrap = CONV_WIDTH - 1
    ext_ref[n_wrap * SUBLANES:, :] = u
    ext_ref[0:n_wrap * SUBLANES, :] = ext_ref[pitch * SUBLANES - 1:(pitch + n_wrap) * SUBLANES - 1, :]
    conv_bias, conv_gain, conv_beta = cb_ref[...], cg_ref[...], cbeta_ref[...]
    vregs_per_chunk = 4
    for a0 in range(0, pitch, vregs_per_chunk):
        accs = [jnp.zeros((SUBLANES, CONV_CHANNELS), F32) for _ in range(vregs_per_chunk)]
        for k in range(CONV_WIDTH):
            tap = cw_ref[(n_wrap - k) * SUBLANES:(n_wrap - k + 1) * SUBLANES, :]
            for i in range(vregs_per_chunk):
                e = n_wrap + a0 + i - k
                accs[i] = accs[i] + ext_ref[e * SUBLANES:(e + 1) * SUBLANES, :] * tap
        y = _layer_norm_rows(jnp.concatenate(accs, axis=0) + conv_bias, conv_gain, conv_beta)
        y = y * jax.nn.sigmoid(y)
        for i in range(vregs_per_chunk):
            for c in range(CONV_CHANNELS // LANES):
                cnat_ref[c, pl.ds(a0 + i, SUBLANES, stride=pitch), :] = (
                    y[i * SUBLANES:(i + 1) * SUBLANES, c * LANES:(c + 1) * LANES])
    c_ref[...] = jnp.concatenate([cnat_ref[c, CONV_HALO:CONV_HALO + tm, :]
                                  for c in range(CONV_CHANNELS // LANES)], axis=1).astype(BF16)

    def rope(h, cos_t, sin_t):
        outs = []
        for c in range(h.shape[1] // LANES):
            hs = h[:, c * LANES:(c + 1) * LANES]
            partner = jnp.where(first_half,
                                pltpu.roll(hs, LANES - HEAD_DIM // 2, 1),
                                pltpu.roll(hs, HEAD_DIM // 2, 1))
            outs.append(hs * cos_t + partner * sin_t)
        return jnp.concatenate(outs, axis=1)

    def store_residue_view(ref, h, d):
        rows = tm // d
        for r in range(d):
            ref[0, :, r * ATTN_WIDTH:(r + 1) * ATTN_WIDTH] = h[r * rows:(r + 1) * rows, :].astype(BF16)

    for g, (_, d) in enumerate(DILATED_GROUPS):
        base = g * 3 * ATTN_WIDTH
        xg = residue_major(d)
        cos_t = table_rows(cos_ref, d)
        sin_t = table_rows(sin_ref, d)
        q = rope(mm(xg, base, ATTN_WIDTH), cos_t, sin_t) * (LOG2_E * HEAD_DIM ** -0.5)
        store_residue_view(qkv_refs[3 * g], q, d)
        store_residue_view(qkv_refs[3 * g + 1], rope(mm(xg, base + ATTN_WIDTH, ATTN_WIDTH), cos_t, sin_t), d)
        store_residue_view(qkv_refs[3 * g + 2], mm(xg, base + 2 * ATTN_WIDTH, ATTN_WIDTH), d)

    ga_ref[...] = jax.nn.sigmoid(mm(xb, GATE_COL0, D_MODEL)).astype(BF16)
    gc_ref[...] = jax.nn.sigmoid(mm(xb, GATE_COL0 + D_MODEL, D_MODEL)).astype(BF16)


def _in_proj(x2, w_in_b, cos_t, sin_t, conv_taps, conv_b, conv_ln_g, conv_ln_b, batch, seq, tm):
    tokens = x2.shape[0]
    n_seq_tiles = seq // tm
    halo_blocks_per_tile = tm // CONV_HALO
    window = CONV_HALO + tm
    assert window % SUBLANES == 0 and (window // SUBLANES) % 8 != 0
    row = lambda i: (i, 0)
    const = lambda i: (0, 0)
    tab = lambda i: (i % n_seq_tiles, 0)
    view = lambda i: (i // n_seq_tiles, i % n_seq_tiles, 0)
    vec = pl.BlockSpec((1, CONV_CHANNELS), const)
    qkv_shapes, qkv_specs = [], []
    for _, d in DILATED_GROUPS:
        qkv_shapes += [jax.ShapeDtypeStruct((batch, seq // d, d * ATTN_WIDTH), BF16)] * 3
        qkv_specs += [pl.BlockSpec((1, tm // d, d * ATTN_WIDTH), view)] * 3
    out_shape = (qkv_shapes
                 + [jax.ShapeDtypeStruct((tokens, CONV_CHANNELS), BF16)]
                 + [jax.ShapeDtypeStruct((tokens, D_MODEL), BF16)] * 2)
    out_specs = (qkv_specs
                 + [pl.BlockSpec((tm, CONV_CHANNELS), row)]
                 + [pl.BlockSpec((tm, D_MODEL), row)] * 2)
    return pl.pallas_call(
        functools.partial(_in_proj_kernel, n_seq_tiles=n_seq_tiles),
        out_shape=out_shape,
        grid=(tokens // tm,),
        in_specs=[
            pl.BlockSpec((tm, D_MODEL), row),
            pl.BlockSpec((CONV_HALO, D_MODEL), lambda i: (jnp.maximum(i * halo_blocks_per_tile - 1, 0), 0)),
            pl.BlockSpec(w_in_b.shape, const, pipeline_mode=pl.Buffered(1)),
            pl.BlockSpec((tm, LANES), tab),
            pl.BlockSpec((tm, LANES), tab),
            pl.BlockSpec((CONV_WIDTH * SUBLANES, CONV_CHANNELS), const),
            vec, vec, vec,
        ],
        out_specs=out_specs,
        scratch_shapes=[pltpu.VMEM((D_MODEL // LANES, window, LANES), F32),
                        pltpu.VMEM(((CONV_WIDTH - 1) * SUBLANES + window, CONV_CHANNELS), F32),
                        pltpu.VMEM((CONV_CHANNELS // LANES, window, LANES), F32)],
        compiler_params=pltpu.CompilerParams(
            dimension_semantics=("parallel",), vmem_limit_bytes=VMEM_LIMIT_BYTES),
        name="in_proj",
    )(x2, x2, w_in_b, cos_t, sin_t, conv_taps, conv_b, conv_ln_g, conv_ln_b)


def _attn_kernel(q_ref, kc_ref, kp_ref, vc_ref, vp_ref, o_ref, stat_ref, bias_ref):
    tq = q_ref.shape[1]
    tile = pl.program_id(2)

    def band(prev_ref, cur_ref, j, cols):
        if j == 0:
            return jnp.concatenate([prev_ref[0, :, cols], cur_ref[0, 0:Q_BLOCK, cols]], axis=0)
        return cur_ref[0, (j - 1) * Q_BLOCK:(j + 1) * Q_BLOCK, cols]

    rows2 = HEADS_PER_VREG * Q_BLOCK
    keys = 2 * Q_BLOCK
    qi = lax.broadcasted_iota(jnp.int32, (rows2, keys), 0) & (Q_BLOCK - 1)
    kk = lax.broadcasted_iota(jnp.int32, (rows2, keys), 1)
    dist = qi + Q_BLOCK - kk
    in_window = (dist >= 0) & (dist <= Q_BLOCK)
    bias_ref[0] = jnp.where(in_window, 0.0, MASK_VALUE)
    bias_ref[1] = jnp.where(in_window & (kk >= Q_BLOCK), 0.0, MASK_VALUE)

    lane = lax.broadcasted_iota(jnp.int32, (Q_BLOCK, LANES), 1)
    head_a = lane < HEAD_DIM
    stat_head = lane // STAT_LANES_PER_HEAD
    stat_is_max = (lane % STAT_LANES_PER_HEAD) < STAT_LANES_PER_HEAD // 2
    ones_cols = jnp.ones((keys, LANES), BF16)

    n_residues = q_ref.shape[2] // ATTN_WIDTH
    for res, j in [(res, j) for res in range(n_residues) for j in range(tq // Q_BLOCK)]:
        r0 = j * Q_BLOCK
        bias = bias_ref[(tile == 0).astype(jnp.int32)] if j == 0 else bias_ref[0]
        stat_tile = jnp.zeros((Q_BLOCK, LANES), F32)
        for hp in range(N_HEAD_PAIRS):
            cols = slice(res * ATTN_WIDTH + hp * LANES, res * ATTN_WIDTH + (hp + 1) * LANES)
            q2 = q_ref[0, r0:r0 + Q_BLOCK, cols]
            zero = jnp.zeros_like(q2)
            qs = jnp.concatenate([jnp.where(head_a, q2, zero), jnp.where(head_a, zero, q2)], axis=0)
            k2 = band(kp_ref, kc_ref, j, cols)
            v2 = band(vp_ref, vc_ref, j, cols)
            s = lax.dot_general(qs, k2, (((1,), (1,)), ((), ())), preferred_element_type=F32) + bias
            m = jnp.max(s, axis=1, keepdims=True)
            p = jnp.exp2(s - m).astype(BF16)
            ov = jnp.dot(p, jnp.concatenate([v2, ones_cols], axis=1), preferred_element_type=F32)
            o_ref[0, r0:r0 + Q_BLOCK, cols] = jnp.where(
                head_a, ov[:Q_BLOCK, :LANES], ov[Q_BLOCK:, :LANES]).astype(BF16)
            m_lanes = jnp.broadcast_to(m, (rows2, LANES))
            for half in range(HEADS_PER_VREG):
                rows = slice(half * Q_BLOCK, (half + 1) * Q_BLOCK)
                stat = jnp.where(stat_is_max, m_lanes[rows], ov[rows, LANES:])
                stat_tile = jnp.where(stat_head == HEADS_PER_VREG * hp + half, stat, stat_tile)
        stat_ref[0, r0:r0 + Q_BLOCK, res * LANES:(res + 1) * LANES] = stat_tile


def _attention_group(q3, k3, v3, dilation, queries_per_step):
    batch, sub_len, _ = q3.shape
    tq = min(queries_per_step, sub_len)
    n_res = queries_per_step // tq
    assert dilation % n_res == 0 and sub_len % tq == 0
    prev_blocks_per_tile = tq // Q_BLOCK
    cur = lambda b, r, n: (b, n, r)
    prev = lambda b, r, n: (b, jnp.maximum(n * prev_blocks_per_tile - 1, 0), r)
    return pl.pallas_call(
        _attn_kernel,
        out_shape=[jax.ShapeDtypeStruct(q3.shape, BF16),
                   jax.ShapeDtypeStruct((batch, sub_len, dilation * LANES), F32)],
        grid=(batch, dilation // n_res, sub_len // tq),
        in_specs=[
            pl.BlockSpec((1, tq, n_res * ATTN_WIDTH), cur),
            pl.BlockSpec((1, tq, n_res * ATTN_WIDTH), cur),
            pl.BlockSpec((1, Q_BLOCK, n_res * ATTN_WIDTH), prev),
            pl.BlockSpec((1, tq, n_res * ATTN_WIDTH), cur),
            pl.BlockSpec((1, Q_BLOCK, n_res * ATTN_WIDTH), prev),
        ],
        out_specs=[pl.BlockSpec((1, tq, n_res * ATTN_WIDTH), cur),
                   pl.BlockSpec((1, tq, n_res * LANES), cur)],
        scratch_shapes=[pltpu.VMEM((2, HEADS_PER_VREG * Q_BLOCK, 2 * Q_BLOCK), F32)],
        compiler_params=pltpu.CompilerParams(
            dimension_semantics=("parallel", "parallel", "arbitrary"),
            vmem_limit_bytes=VMEM_LIMIT_BYTES),
        name=f"attn_d{dilation}",
    )(q3, k3, k3, v3, v3)


def _mix_out_kernel(o1_ref, o2_ref, o3_ref, l1_ref, l2_ref, l3_ref, c_ref, ga_ref, gc_ref, x_ref,
                    wa_ref, wc_ref, wo_ref, g_ref, b_ref, out_ref, o_nat_ref, l_nat_ref, *, alpha, sub_rows):
    tm = x_ref.shape[0]
    n_slabs = ATTN_WIDTH // LANES
    for t0 in range(0, tm, sub_rows):
        rows_nat = slice(t0, t0 + sub_rows)

        def token_major(g, o_ref, l_ref, d):
            rows_view = slice(t0 // d, (t0 + sub_rows) // d)
            if d == 1:
                return o_ref[0, rows_view, :].astype(F32), l_ref[0, rows_view, :]
            for r in range(d):
                rows = pl.ds(t0 + r, sub_rows // d, stride=d)
                piece = o_ref[0, rows_view, r * ATTN_WIDTH:(r + 1) * ATTN_WIDTH].astype(F32)
                for c in range(n_slabs):
                    o_nat_ref[g, c, rows, :] = piece[:, c * LANES:(c + 1) * LANES]
                l_nat_ref[g, rows, :] = l_ref[0, rows_view, r * LANES:(r + 1) * LANES]
            return (jnp.concatenate([o_nat_ref[g, c, rows_nat, :] for c in range(n_slabs)], axis=1),
                    l_nat_ref[g, rows_nat, :])

        os_, stats = [], []
        for g, (o_ref, l_ref) in enumerate(((o1_ref, l1_ref), (o2_ref, l2_ref), (o3_ref, l3_ref))):
            o, stat = token_major(g, o_ref, l_ref, DILATED_GROUPS[g][1])
            os_.append(o)
            stats.append(stat)
        lane = lax.broadcasted_iota(jnp.int32, (sub_rows, LANES), 1)
        is_max_lane = (lane % STAT_LANES_PER_HEAD) < STAT_LANES_PER_HEAD // 2
        m = jnp.maximum(jnp.maximum(stats[0], stats[1]), stats[2])
        es = [jnp.exp2(stat - m) for stat in stats]
        dens = [pltpu.roll(stat, LANES - STAT_LANES_PER_HEAD // 2, 1) for stat in stats]
        inv = 1.0 / (es[0] * dens[0] + es[1] * dens[1] + es[2] * dens[2])
        src = lax.broadcasted_iota(jnp.int32, (2 * LANES, ATTN_WIDTH), 0) % LANES
        dst_head = lax.broadcasted_iota(jnp.int32, (2 * LANES, ATTN_WIDTH), 1) // HEAD_DIM
        spread = jnp.where(src == dst_head * STAT_LANES_PER_HEAD, 1.0, 0.0).astype(BF16)
        o_mix = jnp.zeros((sub_rows, ATTN_WIDTH), F32)
        for e, o in zip(es, os_):
            w = jnp.where(is_max_lane, e * inv, 0.0)
            w_hi = w.astype(BF16)
            w_lo = (w - w_hi.astype(F32)).astype(BF16)
            w_full = jnp.dot(jnp.concatenate([w_hi, w_lo], axis=1), spread, preferred_element_type=F32)
            o_mix = o_mix + w_full * o
        y_attn = jnp.dot(o_mix.astype(BF16), wa_ref[...], preferred_element_type=F32)
        y_conv = jnp.dot(c_ref[rows_nat, :], wc_ref[...], preferred_element_type=F32)
        merged = ga_ref[rows_nat, :].astype(F32) * y_attn + gc_ref[rows_nat, :].astype(F32) * y_conv
        mix = jnp.dot(merged.astype(BF16), wo_ref[...], preferred_element_type=F32)
        out_ref[rows_nat, :] = _layer_norm_rows(alpha * x_ref[rows_nat, :] + mix, g_ref[...], b_ref[...])


def _mix_out(os_, stats, c, ga, gc, x2, wa, wc, wo, ln_g, ln_b, alpha, seq, tm):
    tokens = x2.shape[0]
    n_seq_tiles = seq // tm
    row = lambda i: (i, 0)
    view = lambda i: (i // n_seq_tiles, i % n_seq_tiles, 0)
    const = lambda i: (0, 0)
    resident = lambda a: pl.BlockSpec(a.shape, const, pipeline_mode=pl.Buffered(1))
    o_specs = [pl.BlockSpec((1, tm // d, d * ATTN_WIDTH), view) for _, d in DILATED_GROUPS]
    l_specs = [pl.BlockSpec((1, tm // d, d * LANES), view) for _, d in DILATED_GROUPS]
    return pl.pallas_call(
        functools.partial(_mix_out_kernel, alpha=alpha, sub_rows=tm // 2),
        out_shape=jax.ShapeDtypeStruct((tokens, D_MODEL), F32),
        grid=(tokens // tm,),
        in_specs=(o_specs + l_specs
                  + [pl.BlockSpec((tm, CONV_CHANNELS), row)]
                  + [pl.BlockSpec((tm, D_MODEL), row)] * 3
                  + [resident(wa), resident(wc), resident(wo), resident(ln_g), resident(ln_b)]),
        out_specs=pl.BlockSpec((tm, D_MODEL), row),
        scratch_shapes=[pltpu.VMEM((N_GROUPS, ATTN_WIDTH // LANES, tm, LANES), F32),
                        pltpu.VMEM((N_GROUPS, tm, LANES), F32)],
        compiler_params=pltpu.CompilerParams(
            dimension_semantics=("parallel",), vmem_limit_bytes=VMEM_LIMIT_BYTES),
        name="mix_out",
    )(*os_, *stats, c, ga, gc, x2, wa, wc, wo, ln_g, ln_b)


def _ffn_kernel(x_ref, wg_ref, wu_ref, wd_ref, g_ref, b_ref, out_ref, h_ref, *, alpha, chunk, sub_rows):
    d_ff = wg_ref.shape[1]
    for t0 in range(0, x_ref.shape[0], sub_rows):
        rows = slice(t0, t0 + sub_rows)
        x = x_ref[rows, :]
        xb = x.astype(BF16)
        for c0 in range(0, d_ff, chunk):
            gate = jnp.dot(xb, wg_ref[:, c0:c0 + chunk], preferred_element_type=F32)
            up = jnp.dot(xb, wu_ref[:, c0:c0 + chunk], preferred_element_type=F32)
            h_ref[rows, c0:c0 + chunk] = (gate * jax.nn.sigmoid(gate) * up).astype(BF16)
        ff = jnp.dot(h_ref[rows, :], wd_ref[...], preferred_element_type=F32)
        out_ref[rows, :] = _layer_norm_rows(alpha * x + ff, g_ref[...], b_ref[...])


def _ffn(x1, wg, wu, wd, ln_g, ln_b, alpha, tm):
    tokens = x1.shape[0]
    d_ff = wg.shape[1]
    row = lambda i: (i, 0)
    const = lambda i: (0, 0)
    resident = lambda a: pl.BlockSpec(a.shape, const, pipeline_mode=pl.Buffered(1))
    return pl.pallas_call(
        functools.partial(_ffn_kernel, alpha=alpha, chunk=256, sub_rows=tm // 2),
        out_shape=jax.ShapeDtypeStruct((tokens, D_MODEL), F32),
        grid=(tokens // tm,),
        in_specs=[pl.BlockSpec((tm, D_MODEL), row),
                  resident(wg), resident(wu), resident(wd), resident(ln_g), resident(ln_b)],
        out_specs=pl.BlockSpec((tm, D_MODEL), row),
        scratch_shapes=[pltpu.VMEM((tm, d_ff), BF16)],
        compiler_params=pltpu.CompilerParams(
            dimension_semantics=("parallel",), vmem_limit_bytes=VMEM_LIMIT_BYTES),
        name="ffn",
    )(x1, wg, wu, wd, ln_g, ln_b)


def _rotary_tables(seq):
    inv_freq = 1.0 / (ROPE_THETA ** (jnp.arange(0, HEAD_DIM, 2, dtype=F32) / HEAD_DIM))
    hi = (jnp.arange(seq // ROPE_SPLIT, dtype=F32) * ROPE_SPLIT)[:, None] * inv_freq[None, :]
    lo = jnp.arange(ROPE_SPLIT, dtype=F32)[:, None] * inv_freq[None, :]
    cos_hi, sin_hi, cos_lo, sin_lo = lax.optimization_barrier(
        (jnp.cos(hi)[:, None, :], jnp.sin(hi)[:, None, :], jnp.cos(lo)[None], jnp.sin(lo)[None]))
    cos = (cos_hi * cos_lo - sin_hi * sin_lo).reshape(seq, HEAD_DIM // 2)
    sin = (sin_hi * cos_lo + cos_hi * sin_lo).reshape(seq, HEAD_DIM // 2)
    cos_t = jnp.concatenate([cos, cos] * HEADS_PER_VREG, axis=1)
    sin_t = jnp.concatenate([-sin, sin] * HEADS_PER_VREG, axis=1)
    return cos_t, sin_t


def kernel(x, w_in, conv_w, conv_b, conv_ln_g, conv_ln_b, w_attn_out, w_conv_out, w_o, ln1_g, ln1_b,
           w_ffn_gate, w_ffn_up, w_ffn_down, ln2_g, ln2_b):
    batch, seq, d_model = x.shape
    depth = w_in.shape[0]
    alpha = (2 * depth) ** 0.25
    tokens = batch * seq
    tm = 512
    cos_t, sin_t = _rotary_tables(seq)
    vec = lambda a: a.reshape(1, -1)
    h = x.reshape(tokens, d_model)
    for l in range(depth):
        conv_taps = jnp.repeat(conv_w[l], SUBLANES, axis=0)
        outs = _in_proj(h, w_in[l].astype(BF16), cos_t, sin_t, conv_taps, vec(conv_b[l]),
                        vec(conv_ln_g[l]), vec(conv_ln_b[l]), batch, seq, tm)
        qkv, (c, ga, gc) = outs[:3 * N_GROUPS], outs[3 * N_GROUPS:]
        os_, stats = [], []
        for g, (_, dilation) in enumerate(DILATED_GROUPS):
            o, stat = _attention_group(qkv[3 * g], qkv[3 * g + 1], qkv[3 * g + 2], dilation,
                                       queries_per_step=2 * tm)
            os_.append(o)
            stats.append(stat)
        h1 = _mix_out(os_, stats, c, ga, gc, h, w_attn_out[l].astype(BF16), w_conv_out[l].astype(BF16),
                      w_o[l].astype(BF16), vec(ln1_g[l]), vec(ln1_b[l]), alpha, seq, 2 * tm)
        h = _ffn(h1, w_ffn_gate[l].astype(BF16), w_ffn_up[l].astype(BF16), w_ffn_down[l].astype(BF16),
                 vec(ln2_g[l]), vec(ln2_b[l]), alpha, 2 * tm)
    return h.reshape(batch, seq, d_model)
```

```python
import functools

import jax
import jax.numpy as jnp
from jax import lax
from jax.experimental import pallas as pl
from jax.experimental.pallas import tpu as pltpu

D_MODEL = 1024
HEAD_DIM = 64
N_HEADS = 8
ATTN_WIDTH = N_HEADS * HEAD_DIM
DILATED_GROUPS = ((128, 1), (512, 4), (2048, 16))
N_GROUPS = len(DILATED_GROUPS)
CONV_CHANNELS = 512
CONV_WIDTH = 31
ROPE_THETA = 10000.0
LN_EPS = 1e-5
QKV_COLS = N_GROUPS * 3 * ATTN_WIDTH
CONV_COL0 = QKV_COLS
GATE_COL0 = QKV_COLS + 2 * CONV_CHANNELS

LANES = 128
SUBLANES = 8
HEADS_PER_VREG = LANES // HEAD_DIM
N_HEAD_PAIRS = N_HEADS // HEADS_PER_VREG
STAT_LANES_PER_HEAD = LANES // N_HEADS
LOG2_E = 1.4426950408889634
ROPE_SPLIT = 128
Q_BLOCK = 128
CONV_HALO = 32
MASK_VALUE = -1e30
VMEM_LIMIT_BYTES = 56 * 1024 * 1024

F32 = jnp.float32
BF16 = jnp.bfloat16

assert all(window // dilation == Q_BLOCK for window, dilation in DILATED_GROUPS)


def _layer_norm_rows(h, g, b):
    mu = jnp.mean(h, axis=-1, keepdims=True)
    d = h - mu
    var = jnp.mean(d * d, axis=-1, keepdims=True)
    return d * lax.rsqrt(var + LN_EPS) * g + b


def _in_proj_kernel(x_ref, xh_ref, w_ref, cos_ref, sin_ref, cw_ref, cb_ref, cg_ref, cbeta_ref, *rest,
                    n_seq_tiles):
    out_refs, (slab_ref, ext_ref, cnat_ref) = rest[:-3], rest[-3:]
    qkv_refs = out_refs[:3 * N_GROUPS]
    c_ref, ga_ref, gc_ref = out_refs[3 * N_GROUPS:]
    tm = x_ref.shape[0]
    n_slabs = x_ref.shape[1] // LANES
    xb = x_ref[...].astype(BF16)
    first_tile = (pl.program_id(0) % n_seq_tiles) == 0
    halo = jnp.where(first_tile, 0.0, xh_ref[...])
    for c in range(n_slabs):
        slab_ref[c, 0:CONV_HALO, :] = halo[:, c * LANES:(c + 1) * LANES]
        slab_ref[c, CONV_HALO:CONV_HALO + tm, :] = x_ref[:, c * LANES:(c + 1) * LANES]
    lane = lax.broadcasted_iota(jnp.int32, (tm, LANES), 1)
    first_half = (lane & (HEAD_DIM // 2)) == 0

    def residue_major(d):
        if d == 1:
            return xb
        rows = [jnp.concatenate([slab_ref[c, pl.ds(CONV_HALO + r, tm // d, stride=d), :]
                                 for c in range(n_slabs)], axis=1) for r in range(d)]
        return jnp.concatenate(rows, axis=0).astype(BF16)

    def table_rows(ref, d):
        if d == 1:
            return ref[...]
        return jnp.concatenate([ref[pl.ds(r, tm // d, stride=d), :] for r in range(d)], axis=0)

    def mm(lhs, col0, width):
        return jnp.dot(lhs, w_ref[:, col0:col0 + width], preferred_element_type=F32)

    window = CONV_HALO + tm
    pitch = window // SUBLANES
    x_win = jnp.concatenate(
        [jnp.concatenate([slab_ref[c, pl.ds(a, SUBLANES, stride=pitch), :] for c in range(n_slabs)], axis=1)
         for a in range(pitch)], axis=0).astype(BF16)
    u = mm(x_win, CONV_COL0, CONV_CHANNELS) * jax.nn.sigmoid(mm(x_win, CONV_COL0 + CONV_CHANNELS, CONV_CHANNELS))
    n_wrap = CONV_WIDTH - 1
    ext_ref[n_wrap * SUBLANES:, :] = u
    ext_ref[0:n_wrap * SUBLANES, :] = ext_ref[pitch * SUBLANES - 1:(pitch + n_wrap) * SUBLANES - 1, :]
    conv_bias, conv_gain, conv_beta = cb_ref[...], cg_ref[...], cbeta_ref[...]
    vregs_per_chunk = 4
    for a0 in range(0, pitch, vregs_per_chunk):
        accs = [jnp.zeros((SUBLANES, CONV_CHANNELS), F32) for _ in range(vregs_per_chunk)]
        for k in range(CONV_WIDTH):
            tap = cw_ref[(n_wrap - k) * SUBLANES:(n_wrap - k + 1) * SUBLANES, :]
            for i in range(vregs_per_chunk):
                e = n_wrap + a0 + i - k
                accs[i] = accs[i] + ext_ref[e * SUBLANES:(e + 1) * SUBLANES, :] * tap
        y = _layer_norm_rows(jnp.concatenate(accs, axis=0) + conv_bias, conv_gain, conv_beta)
        y = y * jax.nn.sigmoid(y)
        for i in range(vregs_per_chunk):
            for c in range(CONV_CHANNELS // LANES):
                cnat_ref[c, pl.ds(a0 + i, SUBLANES, stride=pitch), :] = (
                    y[i * SUBLANES:(i + 1) * SUBLANES, c * LANES:(c + 1) * LANES])
    c_ref[...] = jnp.concatenate([cnat_ref[c, CONV_HALO:CONV_HALO + tm, :]
                                  for c in range(CONV_CHANNELS // LANES)], axis=1).astype(BF16)

    def rope(h, cos_t, sin_t):
        outs = []
        for c in range(h.shape[1] // LANES):
            hs = h[:, c * LANES:(c + 1) * LANES]
            partner = jnp.where(first_half,
                                pltpu.roll(hs, LANES - HEAD_DIM // 2, 1),
                                pltpu.roll(hs, HEAD_DIM // 2, 1))
            outs.append(hs * cos_t + partner * sin_t)
        return jnp.concatenate(outs, axis=1)

    def store_residue_view(ref, h, d):
        rows = tm // d
        for r in range(d):
            ref[0, :, r * ATTN_WIDTH:(r + 1) * ATTN_WIDTH] = h[r * rows:(r + 1) * rows, :].astype(BF16)

    for g, (_, d) in enumerate(DILATED_GROUPS):
        base = g * 3 * ATTN_WIDTH
        xg = residue_major(d)
        cos_t = table_rows(cos_ref, d)
        sin_t = table_rows(sin_ref, d)
        q = rope(mm(xg, base, ATTN_WIDTH), cos_t, sin_t) * (LOG2_E * HEAD_DIM ** -0.5)
        store_residue_view(qkv_refs[3 * g], q, d)
        store_residue_view(qkv_refs[3 * g + 1], rope(mm(xg, base + ATTN_WIDTH, ATTN_WIDTH), cos_t, sin_t), d)
        store_residue_view(qkv_refs[3 * g + 2], mm(xg, base + 2 * ATTN_WIDTH, ATTN_WIDTH), d)

    ga_ref[...] = jax.nn.sigmoid(mm(xb, GATE_COL0, D_MODEL)).astype(BF16)
    gc_ref[...] = jax.nn.sigmoid(mm(xb, GATE_COL0 + D_MODEL, D_MODEL)).astype(BF16)


def _in_proj(x2, w_in_b, cos_t, sin_t, conv_taps, conv_b, conv_ln_g, conv_ln_b, batch, seq, tm):
    tokens = x2.shape[0]
    n_seq_tiles = seq // tm
    halo_blocks_per_tile = tm // CONV_HALO
    window = CONV_HALO + tm
    assert window % SUBLANES == 0 and (window // SUBLANES) % 8 != 0
    row = lambda i: (i, 0)
    const = lambda i: (0, 0)
    tab = lambda i: (i % n_seq_tiles, 0)
    view = lambda i: (i // n_seq_tiles, i % n_seq_tiles, 0)
    vec = pl.BlockSpec((1, CONV_CHANNELS), const)
    qkv_shapes, qkv_specs = [], []
    for _, d in DILATED_GROUPS:
        qkv_shapes += [jax.ShapeDtypeStruct((batch, seq // d, d * ATTN_WIDTH), BF16)] * 3
        qkv_specs += [pl.BlockSpec((1, tm // d, d * ATTN_WIDTH), view)] * 3
    out_shape = (qkv_shapes
                 + [jax.ShapeDtypeStruct((tokens, CONV_CHANNELS), BF16)]
                 + [jax.ShapeDtypeStruct((tokens, D_MODEL), BF16)] * 2)
    out_specs = (qkv_specs
                 + [pl.BlockSpec((tm, CONV_CHANNELS), row)]
                 + [pl.BlockSpec((tm, D_MODEL), row)] * 2)
    return pl.pallas_call(
        functools.partial(_in_proj_kernel, n_seq_tiles=n_seq_tiles),
        out_shape=out_shape,
        grid=(tokens // tm,),
        in_specs=[
            pl.BlockSpec((tm, D_MODEL), row),
            pl.BlockSpec((CONV_HALO, D_MODEL), lambda i: (jnp.maximum(i * halo_blocks_per_tile - 1, 0), 0)),
            pl.BlockSpec(w_in_b.shape, const, pipeline_mode=pl.Buffered(1)),
            pl.BlockSpec((tm, LANES), tab),
            pl.BlockSpec((tm, LANES), tab),
            pl.BlockSpec((CONV_WIDTH * SUBLANES, CONV_CHANNELS), const),
            vec, vec, vec,
        ],
        out_specs=out_specs,
        scratch_shapes=[pltpu.VMEM((D_MODEL // LANES, window, LANES), F32),
                        pltpu.VMEM(((CONV_WIDTH - 1) * SUBLANES + window, CONV_CHANNELS), F32),
                        pltpu.VMEM((CONV_CHANNELS // LANES, window, LANES), F32)],
        compiler_params=pltpu.CompilerParams(
            dimension_semantics=("parallel",), vmem_limit_bytes=VMEM_LIMIT_BYTES),
        name="in_proj",
    )(x2, x2, w_in_b, cos_t, sin_t, conv_taps, conv_b, conv_ln_g, conv_ln_b)


def _attn_kernel(q_ref, kc_ref, kp_ref, vc_ref, vp_ref, o_ref, stat_ref, bias_ref):
    tq = q_ref.shape[1]
    tile = pl.program_id(2)

    def band(prev_ref, cur_ref, j, cols):
        if j == 0:
            return jnp.concatenate([prev_ref[0, :, cols], cur_ref[0, 0:Q_BLOCK, cols]], axis=0)
        return cur_ref[0, (j - 1) * Q_BLOCK:(j + 1) * Q_BLOCK, cols]

    rows2 = HEADS_PER_VREG * Q_BLOCK
    keys = 2 * Q_BLOCK
    qi = lax.broadcasted_iota(jnp.int32, (rows2, keys), 0) & (Q_BLOCK - 1)
    kk = lax.broadcasted_iota(jnp.int32, (rows2, keys), 1)
    dist = qi + Q_BLOCK - kk
    in_window = (dist >= 0) & (dist <= Q_BLOCK)
    bias_ref[0] = jnp.where(in_window, 0.0, MASK_VALUE)
    bias_ref[1] = jnp.where(in_window & (kk >= Q_BLOCK), 0.0, MASK_VALUE)

    lane = lax.broadcasted_iota(jnp.int32, (Q_BLOCK, LANES), 1)
    head_a = lane < HEAD_DIM
    stat_head = lane // STAT_LANES_PER_HEAD
    stat_is_max = (lane % STAT_LANES_PER_HEAD) < STAT_LANES_PER_HEAD // 2
    ones_cols = jnp.ones((keys, LANES), BF16)

    n_residues = q_ref.shape[2] // ATTN_WIDTH
    for res, j in [(res, j) for res in range(n_residues) for j in range(tq // Q_BLOCK)]:
        r0 = j * Q_BLOCK
        bias = bias_ref[(tile == 0).astype(jnp.int32)] if j == 0 else bias_ref[0]
        stat_tile = jnp.zeros((Q_BLOCK, LANES), F32)
        for hp in range(N_HEAD_PAIRS):
            cols = slice(res * ATTN_WIDTH + hp * LANES, res * ATTN_WIDTH + (hp + 1) * LANES)
            q2 = q_ref[0, r0:r0 + Q_BLOCK, cols]
            zero = jnp.zeros_like(q2)
            qs = jnp.concatenate([jnp.where(head_a, q2, zero), jnp.where(head_a, zero, q2)], axis=0)
            k2 = band(kp_ref, kc_ref, j, cols)
            v2 = band(vp_ref, vc_ref, j, cols)
            s = lax.dot_general(qs, k2, (((1,), (1,)), ((), ())), preferred_element_type=F32) + bias
            m = jnp.max(s, axis=1, keepdims=True)
            p = jnp.exp2(s - m).astype(BF16)
            ov = jnp.dot(p, jnp.concatenate([v2, ones_cols], axis=1), preferred_element_type=F32)
            o_ref[0, r0:r0 + Q_BLOCK, cols] = jnp.where(
                head_a, ov[:Q_BLOCK, :LANES], ov[Q_BLOCK:, :LANES]).astype(BF16)
            m_lanes = jnp.broadcast_to(m, (rows2, LANES))
            for half in range(HEADS_PER_VREG):
                rows = slice(half * Q_BLOCK, (half + 1) * Q_BLOCK)
                stat = jnp.where(stat_is_max, m_lanes[rows], ov[rows, LANES:])
                stat_tile = jnp.where(stat_head == HEADS_PER_VREG * hp + half, stat, stat_tile)
        stat_ref[0, r0:r0 + Q_BLOCK, res * LANES:(res + 1) * LANES] = stat_tile


def _attention_group(q3, k3, v3, dilation, queries_per_step):
    batch, sub_len, _ = q3.shape
    tq = min(queries_per_step, sub_len)
    n_res = queries_per_step // tq
    assert dilation % n_res == 0 and sub_len % tq == 0
    prev_blocks_per_tile = tq // Q_BLOCK
    cur = lambda b, r, n: (b, n, r)
    prev = lambda b, r, n: (b, jnp.maximum(n * prev_blocks_per_tile - 1, 0), r)
    return pl.pallas_call(
        _attn_kernel,
        out_shape=[jax.ShapeDtypeStruct(q3.shape, BF16),
                   jax.ShapeDtypeStruct((batch, sub_len, dilation * LANES), F32)],
        grid=(batch, dilation // n_res, sub_len // tq),
        in_specs=[
            pl.BlockSpec((1, tq, n_res * ATTN_WIDTH), cur),
            pl.BlockSpec((1, tq, n_res * ATTN_WIDTH), cur),
            pl.BlockSpec((1, Q_BLOCK, n_res * ATTN_WIDTH), prev),
            pl.BlockSpec((1, tq, n_res * ATTN_WIDTH), cur),
            pl.BlockSpec((1, Q_BLOCK, n_res * ATTN_WIDTH), prev),
        ],
        out_specs=[pl.BlockSpec((1, tq, n_res * ATTN_WIDTH), cur),
                   pl.BlockSpec((1, tq, n_res * LANES), cur)],
        scratch_shapes=[pltpu.VMEM((2, HEADS_PER_VREG * Q_BLOCK, 2 * Q_BLOCK), F32)],
        compiler_params=pltpu.CompilerParams(
            dimension_semantics=("parallel", "parallel", "arbitrary"),
            vmem_limit_bytes=VMEM_LIMIT_BYTES),
        name=f"attn_d{dilation}",
    )(q3, k3, k3, v3, v3)


def _mix_out_kernel(o1_ref, o2_ref, o3_ref, l1_ref, l2_ref, l3_ref, c_ref, ga_ref, gc_ref, x_ref,
                    wa_ref, wc_ref, wo_ref, g_ref, b_ref, out_ref, o_nat_ref, l_nat_ref, *, alpha, sub_rows):
    tm = x_ref.shape[0]
    n_slabs = ATTN_WIDTH // LANES
    for t0 in range(0, tm, sub_rows):
        rows_nat = slice(t0, t0 + sub_rows)

        def token_major(g, o_ref, l_ref, d):
            rows_view = slice(t0 // d, (t0 + sub_rows) // d)
            if d == 1:
                return o_ref[0, rows_view, :].astype(F32), l_ref[0, rows_view, :]
            for r in range(d):
                rows = pl.ds(t0 + r, sub_rows // d, stride=d)
                piece = o_ref[0, rows_view, r * ATTN_WIDTH:(r + 1) * ATTN_WIDTH].astype(F32)
                for c in range(n_slabs):
                    o_nat_ref[g, c, rows, :] = piece[:, c * LANES:(c + 1) * LANES]
                l_nat_ref[g, rows, :] = l_ref[0, rows_view, r * LANES:(r + 1) * LANES]
            return (jnp.concatenate([o_nat_ref[g, c, rows_nat, :] for c in range(n_slabs)], axis=1),
                    l_nat_ref[g, rows_nat, :])

        os_, stats = [], []
        for g, (o_ref, l_ref) in enumerate(((o1_ref, l1_ref), (o2_ref, l2_ref), (o3_ref, l3_ref))):
            o, stat = token_major(g, o_ref, l_ref, DILATED_GROUPS[g][1])
            os_.append(o)
            stats.append(stat)
        lane = lax.broadcasted_iota(jnp.int32, (sub_rows, LANES), 1)
        is_max_lane = (lane % STAT_LANES_PER_HEAD) < STAT_LANES_PER_HEAD // 2
        m = jnp.maximum(jnp.maximum(stats[0], stats[1]), stats[2])
        es = [jnp.exp2(stat - m) for stat in stats]
        dens = [pltpu.roll(stat, LANES - STAT_LANES_PER_HEAD // 2, 1) for stat in stats]
        inv = 1.0 / (es[0] * dens[0] + es[1] * dens[1] + es[2] * dens[2])
        src = lax.broadcasted_iota(jnp.int32, (2 * LANES, ATTN_WIDTH), 0) % LANES
        dst_head = lax.broadcasted_iota(jnp.int32, (2 * LANES, ATTN_WIDTH), 1) // HEAD_DIM
        spread = jnp.where(src == dst_head * STAT_LANES_PER_HEAD, 1.0, 0.0).astype(BF16)
        o_mix = jnp.zeros((sub_rows, ATTN_WIDTH), F32)
        for e, o in zip(es, os_):
            w = jnp.where(is_max_lane, e * inv, 0.0)
            w_hi = w.astype(BF16)
            w_lo = (w - w_hi.astype(F32)).astype(BF16)
            w_full = jnp.dot(jnp.concatenate([w_hi, w_lo], axis=1), spread, preferred_element_type=F32)
            o_mix = o_mix + w_full * o
        y_attn = jnp.dot(o_mix.astype(BF16), wa_ref[...], preferred_element_type=F32)
        y_conv = jnp.dot(c_ref[rows_nat, :], wc_ref[...], preferred_element_type=F32)
        merged = ga_ref[rows_nat, :].astype(F32) * y_attn + gc_ref[rows_nat, :].astype(F32) * y_conv
        mix = jnp.dot(merged.astype(BF16), wo_ref[...], preferred_element_type=F32)
        out_ref[rows_nat, :] = _layer_norm_rows(alpha * x_ref[rows_nat, :] + mix, g_ref[...], b_ref[...])


def _mix_out(os_, stats, c, ga, gc, x2, wa, wc, wo, ln_g, ln_b, alpha, seq, tm):
    tokens = x2.shape[0]
    n_seq_tiles = seq // tm
    row = lambda i: (i, 0)
    view = lambda i: (i // n_seq_tiles, i % n_seq_tiles, 0)
    const = lambda i: (0, 0)
    resident = lambda a: pl.BlockSpec(a.shape, const, pipeline_mode=pl.Buffered(1))
    o_specs = [pl.BlockSpec((1, tm // d, d * ATTN_WIDTH), view) for _, d in DILATED_GROUPS]
    l_specs = [pl.BlockSpec((1, tm // d, d * LANES), view) for _, d in DILATED_GROUPS]
    return pl.pallas_call(
        functools.partial(_mix_out_kernel, alpha=alpha, sub_rows=tm // 2),
        out_shape=jax.ShapeDtypeStruct((tokens, D_MODEL), F32),
        grid=(tokens // tm,),
        in_specs=(o_specs + l_specs
                  + [pl.BlockSpec((tm, CONV_CHANNELS), row)]
                  + [pl.BlockSpec((tm, D_MODEL), row)] * 3
                  + [resident(wa), resident(wc), resident(wo), resident(ln_g), resident(ln_b)]),
        out_specs=pl.BlockSpec((tm, D_MODEL), row),
        scratch_shapes=[pltpu.VMEM((N_GROUPS, ATTN_WIDTH // LANES, tm, LANES), F32),
                        pltpu.VMEM((N_GROUPS, tm, LANES), F32)],
        compiler_params=pltpu.CompilerParams(
            dimension_semantics=("parallel",), vmem_limit_bytes=VMEM_LIMIT_BYTES),
        name="mix_out",
    )(*os_, *stats, c, ga, gc, x2, wa, wc, wo, ln_g, ln_b)


def _ffn_kernel(x_ref, wg_ref, wu_ref, wd_ref, g_ref, b_ref, out_ref, h_ref, *, alpha, chunk, sub_rows):
    d_ff = wg_ref.shape[1]
    for t0 in range(0, x_ref.shape[0], sub_rows):
        rows = slice(t0, t0 + sub_rows)
        x = x_ref[rows, :]
        xb = x.astype(BF16)
        for c0 in range(0, d_ff, chunk):
            gate = jnp.dot(xb, wg_ref[:, c0:c0 + chunk], preferred_element_type=F32)
            up = jnp.dot(xb, wu_ref[:, c0:c0 + chunk], preferred_element_type=F32)
            h_ref[rows, c0:c0 + chunk] = (gate * jax.nn.sigmoid(gate) * up).astype(BF16)
        ff = jnp.dot(h_ref[rows, :], wd_ref[...], preferred_element_type=F32)
        out_ref[rows, :] = _layer_norm_rows(alpha * x + ff, g_ref[...], b_ref[...])


def _ffn(x1, wg, wu, wd, ln_g, ln_b, alpha, tm):
    tokens = x1.shape[0]
    d_ff = wg.shape[1]
    row = lambda i: (i, 0)
    const = lambda i: (0, 0)
    resident = lambda a: pl.BlockSpec(a.shape, const, pipeline_mode=pl.Buffered(1))
    return pl.pallas_call(
        functools.partial(_ffn_kernel, alpha=alpha, chunk=256, sub_rows=tm // 2),
        out_shape=jax.ShapeDtypeStruct((tokens, D_MODEL), F32),
        grid=(tokens // tm,),
        in_specs=[pl.BlockSpec((tm, D_MODEL), row),
                  resident(wg), resident(wu), resident(wd), resident(ln_g), resident(ln_b)],
        out_specs=pl.BlockSpec((tm, D_MODEL), row),
        scratch_shapes=[pltpu.VMEM((tm, d_ff), BF16)],
        compiler_params=pltpu.CompilerParams(
            dimension_semantics=("parallel",), vmem_limit_bytes=VMEM_LIMIT_BYTES),
        name="ffn",
    )(x1, wg, wu, wd, ln_g, ln_b)


def _rotary_tables(seq):
    inv_freq = 1.0 / (ROPE_THETA ** (jnp.arange(0, HEAD_DIM, 2, dtype=F32) / HEAD_DIM))
    hi = (jnp.arange(seq // ROPE_SPLIT, dtype=F32) * ROPE_SPLIT)[:, None] * inv_freq[None, :]
    lo = jnp.arange(ROPE_SPLIT, dtype=F32)[:, None] * inv_freq[None, :]
    cos_hi, sin_hi, cos_lo, sin_lo = lax.optimization_barrier(
        (jnp.cos(hi)[:, None, :], jnp.sin(hi)[:, None, :], jnp.cos(lo)[None], jnp.sin(lo)[None]))
    cos = (cos_hi * cos_lo - sin_hi * sin_lo).reshape(seq, HEAD_DIM // 2)
    sin = (sin_hi * cos_lo + cos_hi * sin_lo).reshape(seq, HEAD_DIM // 2)
    cos_t = jnp.concatenate([cos, cos] * HEADS_PER_VREG, axis=1)
    sin_t = jnp.concatenate([-sin, sin] * HEADS_PER_VREG, axis=1)
    return cos_t, sin_t


def kernel(x, w_in, conv_w, conv_b, conv_ln_g, conv_ln_b, w_attn_out, w_conv_out, w_o, ln1_g, ln1_b,
           w_ffn_gate, w_ffn_up, w_ffn_down, ln2_g, ln2_b):
    batch, seq, d_model = x.shape
    depth = w_in.shape[0]
    alpha = (2 * depth) ** 0.25
    tokens = batch * seq
    tm = 512
    cos_t, sin_t = _rotary_tables(seq)
    vec = lambda a: a.reshape(1, -1)
    h = x.reshape(tokens, d_model)
    for l in range(depth):
        conv_taps = jnp.repeat(conv_w[l], SUBLANES, axis=0)
        outs = _in_proj(h, w_in[l].astype(BF16), cos_t, sin_t, conv_taps, vec(conv_b[l]),
                        vec(conv_ln_g[l]), vec(conv_ln_b[l]), batch, seq, tm)
        qkv, (c, ga, gc) = outs[:3 * N_GROUPS], outs[3 * N_GROUPS:]
        os_, stats = [], []
        for g, (_, dilation) in enumerate(DILATED_GROUPS):
            o, stat = _attention_group(qkv[3 * g], qkv[3 * g + 1], qkv[3 * g + 2], dilation,
                                       queries_per_step=4 * tm)
            os_.append(o)
            stats.append(stat)
        h1 = _mix_out(os_, stats, c, ga, gc, h, w_attn_out[l].astype(BF16), w_conv_out[l].astype(BF16),
                      w_o[l].astype(BF16), vec(ln1_g[l]), vec(ln1_b[l]), alpha, seq, 2 * tm)
        h = _ffn(h1, w_ffn_gate[l].astype(BF16), w_ffn_up[l].astype(BF16), w_ffn_down[l].astype(BF16),
                 vec(ln2_g[l]), vec(ln2_b[l]), alpha, 2 * tm)
    return h.reshape(batch, seq, d_model)
```

```python
import functools

import jax
import jax.numpy as jnp
from jax import lax
from jax.experimental import pallas as pl
from jax.experimental.pallas import tpu as pltpu

D_MODEL = 1024
HEAD_DIM = 64
N_HEADS = 8
ATTN_WIDTH = N_HEADS * HEAD_DIM
DILATED_GROUPS = ((128, 1), (512, 4), (2048, 16))
N_GROUPS = len(DILATED_GROUPS)
CONV_CHANNELS = 512
CONV_WIDTH = 31
ROPE_THETA = 10000.0
LN_EPS = 1e-5
QKV_COLS = N_GROUPS * 3 * ATTN_WIDTH
CONV_COL0 = QKV_COLS
GATE_COL0 = QKV_COLS + 2 * CONV_CHANNELS

LANES = 128
SUBLANES = 8
PACKED_ROWS = 2 * SUBLANES
HEADS_PER_VREG = LANES // HEAD_DIM
N_HEAD_PAIRS = N_HEADS // HEADS_PER_VREG
STAT_LANES_PER_HEAD = LANES // N_HEADS
LOG2_E = 1.4426950408889634
ROPE_SPLIT = 128
Q_BLOCK = 128
CONV_HALO = 32
MASK_VALUE = -1e30
VMEM_LIMIT_BYTES = 56 * 1024 * 1024

F32 = jnp.float32
BF16 = jnp.bfloat16

assert all(window // dilation == Q_BLOCK for window, dilation in DILATED_GROUPS)


def _layer_norm_rows(h, g, b):
    mu = jnp.mean(h, axis=-1, keepdims=True)
    d = h - mu
    var = jnp.mean(d * d, axis=-1, keepdims=True)
    return d * lax.rsqrt(var + LN_EPS) * g + b


def _in_proj_kernel(x_ref, xh_ref, w_ref, cos_ref, sin_ref, *rest, n_seq_tiles):
    out_refs, (slab_ref, ext_ref) = rest[:-2], rest[-2:]
    qkv_refs = out_refs[:3 * N_GROUPS]
    u_even_ref, u_odd_ref, ga_ref, gc_ref = out_refs[3 * N_GROUPS:]
    tm = x_ref.shape[0]
    n_slabs = x_ref.shape[1] // LANES
    xb = x_ref[...].astype(BF16)
    first_tile = (pl.program_id(0) % n_seq_tiles) == 0
    halo = jnp.where(first_tile, 0.0, xh_ref[...])
    for c in range(n_slabs):
        slab_ref[c, 0:CONV_HALO, :] = halo[:, c * LANES:(c + 1) * LANES]
        slab_ref[c, CONV_HALO:CONV_HALO + tm, :] = x_ref[:, c * LANES:(c + 1) * LANES]
    lane = lax.broadcasted_iota(jnp.int32, (tm, LANES), 1)
    first_half = (lane & (HEAD_DIM // 2)) == 0

    def residue_major(d):
        if d == 1:
            return xb
        rows = [jnp.concatenate([slab_ref[c, pl.ds(CONV_HALO + r, tm // d, stride=d), :]
                                 for c in range(n_slabs)], axis=1) for r in range(d)]
        return jnp.concatenate(rows, axis=0).astype(BF16)

    def table_rows(ref, d):
        if d == 1:
            return ref[...]
        return jnp.concatenate([ref[pl.ds(r, tm // d, stride=d), :] for r in range(d)], axis=0)

    def mm(lhs, col0, width):
        return jnp.dot(lhs, w_ref[:, col0:col0 + width], preferred_element_type=F32)

    window = CONV_HALO + tm
    pitch = window // SUBLANES
    x_win = jnp.concatenate(
        [jnp.concatenate([slab_ref[c, pl.ds(a, SUBLANES, stride=pitch), :] for c in range(n_slabs)], axis=1)
         for a in range(pitch)], axis=0).astype(BF16)
    u = mm(x_win, CONV_COL0, CONV_CHANNELS) * jax.nn.sigmoid(mm(x_win, CONV_COL0 + CONV_CHANNELS, CONV_CHANNELS))
    n_wrap = CONV_WIDTH - 1
    ext_ref[n_wrap * SUBLANES:, :] = u
    ext_ref[0:n_wrap * SUBLANES, :] = ext_ref[pitch * SUBLANES - 1:(pitch + n_wrap) * SUBLANES - 1, :]
    u_even_ref[0] = ext_ref[...].astype(BF16)
    u_odd_ref[0] = ext_ref[SUBLANES:SUBLANES + u_odd_ref.shape[1], :].astype(BF16)

    def rope(h, cos_t, sin_t):
        outs = []
        for c in range(h.shape[1] // LANES):
            hs = h[:, c * LANES:(c + 1) * LANES]
            partner = jnp.where(first_half,
                                pltpu.roll(hs, LANES - HEAD_DIM // 2, 1),
                                pltpu.roll(hs, HEAD_DIM // 2, 1))
            outs.append(hs * cos_t + partner * sin_t)
        return jnp.concatenate(outs, axis=1)

    def store_residue_view(ref, h, d):
        rows = tm // d
        for r in range(d):
            ref[0, :, r * ATTN_WIDTH:(r + 1) * ATTN_WIDTH] = h[r * rows:(r + 1) * rows, :].astype(BF16)

    for g, (_, d) in enumerate(DILATED_GROUPS):
        base = g * 3 * ATTN_WIDTH
        xg = residue_major(d)
        cos_t = table_rows(cos_ref, d)
        sin_t = table_rows(sin_ref, d)
        q = rope(mm(xg, base, ATTN_WIDTH), cos_t, sin_t) * (LOG2_E * HEAD_DIM ** -0.5)
        store_residue_view(qkv_refs[3 * g], q, d)
        store_residue_view(qkv_refs[3 * g + 1], rope(mm(xg, base + ATTN_WIDTH, ATTN_WIDTH), cos_t, sin_t), d)
        store_residue_view(qkv_refs[3 * g + 2], mm(xg, base + 2 * ATTN_WIDTH, ATTN_WIDTH), d)

    ga_ref[...] = jax.nn.sigmoid(mm(xb, GATE_COL0, D_MODEL)).astype(BF16)
    gc_ref[...] = jax.nn.sigmoid(mm(xb, GATE_COL0 + D_MODEL, D_MODEL)).astype(BF16)


def _conv_window_rows(tm):
    return (CONV_WIDTH - 1) * SUBLANES + CONV_HALO + tm


def _in_proj(x2, w_in_b, cos_t, sin_t, batch, seq, tm):
    tokens = x2.shape[0]
    n_tiles = tokens // tm
    n_seq_tiles = seq // tm
    halo_blocks_per_tile = tm // CONV_HALO
    window = CONV_HALO + tm
    assert window % SUBLANES == 0 and (window // SUBLANES) % 8 != 0
    ext_rows = _conv_window_rows(tm)
    row = lambda i: (i, 0)
    const = lambda i: (0, 0)
    tab = lambda i: (i % n_seq_tiles, 0)
    tile3 = lambda i: (i, 0, 0)
    view = lambda i: (i // n_seq_tiles, i % n_seq_tiles, 0)
    qkv_shapes, qkv_specs = [], []
    for _, d in DILATED_GROUPS:
        qkv_shapes += [jax.ShapeDtypeStruct((batch, seq // d, d * ATTN_WIDTH), BF16)] * 3
        qkv_specs += [pl.BlockSpec((1, tm // d, d * ATTN_WIDTH), view)] * 3
    u_rows = (ext_rows, ext_rows - PACKED_ROWS)
    out_shape = (qkv_shapes
                 + [jax.ShapeDtypeStruct((n_tiles, r, CONV_CHANNELS), BF16) for r in u_rows]
                 + [jax.ShapeDtypeStruct((tokens, D_MODEL), BF16)] * 2)
    out_specs = (qkv_specs
                 + [pl.BlockSpec((1, r, CONV_CHANNELS), tile3) for r in u_rows]
                 + [pl.BlockSpec((tm, D_MODEL), row)] * 2)
    return pl.pallas_call(
        functools.partial(_in_proj_kernel, n_seq_tiles=n_seq_tiles),
        out_shape=out_shape,
        grid=(n_tiles,),
        in_specs=[
            pl.BlockSpec((tm, D_MODEL), row),
            pl.BlockSpec((CONV_HALO, D_MODEL), lambda i: (jnp.maximum(i * halo_blocks_per_tile - 1, 0), 0)),
            pl.BlockSpec(w_in_b.shape, const, pipeline_mode=pl.Buffered(1)),
            pl.BlockSpec((tm, LANES), tab),
            pl.BlockSpec((tm, LANES), tab),
        ],
        out_specs=out_specs,
        scratch_shapes=[pltpu.VMEM((D_MODEL // LANES, window, LANES), F32),
                        pltpu.VMEM((ext_rows, CONV_CHANNELS), F32)],
        compiler_params=pltpu.CompilerParams(
            dimension_semantics=("parallel",), vmem_limit_bytes=VMEM_LIMIT_BYTES),
        name="in_proj",
    )(x2, x2, w_in_b, cos_t, sin_t)


def _conv_kernel(u_even_ref, u_odd_ref, cw_ref, cb_ref, cg_ref, cbeta_ref, c_ref, acc_ref, cnat_ref):
    tm = c_ref.shape[0]
    window = CONV_HALO + tm
    pitch = window // SUBLANES
    u_refs = (u_even_ref, u_odd_ref)
    for b in range(pitch // 2):
        acc = None
        for j in range(CONV_WIDTH):
            t = (2 * b + j) // 2
            term = (u_refs[j % 2][0, t * PACKED_ROWS:(t + 1) * PACKED_ROWS, :].astype(F32)
                    * cw_ref[j * PACKED_ROWS:(j + 1) * PACKED_ROWS, :].astype(F32))
            acc = term if acc is None else acc + term
        acc_ref[b * PACKED_ROWS:(b + 1) * PACKED_ROWS, :] = acc
    rows_per_chunk = 4 * SUBLANES
    for r0 in range(0, window, rows_per_chunk):
        y = _layer_norm_rows(acc_ref[r0:r0 + rows_per_chunk, :] + cb_ref[...], cg_ref[...], cbeta_ref[...])
        y = y * jax.nn.sigmoid(y)
        for i in range(rows_per_chunk // SUBLANES):
            for c in range(CONV_CHANNELS // LANES):
                cnat_ref[c, pl.ds(r0 // SUBLANES + i, SUBLANES, stride=pitch), :] = (
                    y[i * SUBLANES:(i + 1) * SUBLANES, c * LANES:(c + 1) * LANES])
    c_ref[...] = jnp.concatenate([cnat_ref[c, CONV_HALO:CONV_HALO + tm, :]
                                  for c in range(CONV_CHANNELS // LANES)], axis=1).astype(BF16)


def _conv_branch(u_even, u_odd, conv_taps, conv_b, ln_g, ln_b, tm):
    n_tiles = u_even.shape[0]
    window = CONV_HALO + tm
    const = lambda i: (0, 0)
    tile3 = lambda i: (i, 0, 0)
    vec = pl.BlockSpec((1, CONV_CHANNELS), const)
    return pl.pallas_call(
        _conv_kernel,
        out_shape=jax.ShapeDtypeStruct((n_tiles * tm, CONV_CHANNELS), BF16),
        grid=(n_tiles,),
        in_specs=[pl.BlockSpec((1,) + u_even.shape[1:], tile3),
                  pl.BlockSpec((1,) + u_odd.shape[1:], tile3),
                  pl.BlockSpec(conv_taps.shape, const),
                  vec, vec, vec],
        out_specs=pl.BlockSpec((tm, CONV_CHANNELS), lambda i: (i, 0)),
        scratch_shapes=[pltpu.VMEM((window, CONV_CHANNELS), F32),
                        pltpu.VMEM((CONV_CHANNELS // LANES, window, LANES), F32)],
        compiler_params=pltpu.CompilerParams(
            dimension_semantics=("parallel",), vmem_limit_bytes=VMEM_LIMIT_BYTES),
        name="conv_branch",
    )(u_even, u_odd, conv_taps, conv_b, ln_g, ln_b)


def _attn_kernel(q_ref, kc_ref, kp_ref, vc_ref, vp_ref, o_ref, stat_ref, bias_ref):
    tq = q_ref.shape[1]
    tile = pl.program_id(2)

    def band(prev_ref, cur_ref, j, cols):
        if j == 0:
            return jnp.concatenate([prev_ref[0, :, cols], cur_ref[0, 0:Q_BLOCK, cols]], axis=0)
        return cur_ref[0, (j - 1) * Q_BLOCK:(j + 1) * Q_BLOCK, cols]

    rows2 = HEADS_PER_VREG * Q_BLOCK
    keys = 2 * Q_BLOCK
    qi = lax.broadcasted_iota(jnp.int32, (rows2, keys), 0) & (Q_BLOCK - 1)
    kk = lax.broadcasted_iota(jnp.int32, (rows2, keys), 1)
    dist = qi + Q_BLOCK - kk
    in_window = (dist >= 0) & (dist <= Q_BLOCK)
    bias_ref[0] = jnp.where(in_window, 0.0, MASK_VALUE)
    bias_ref[1] = jnp.where(in_window & (kk >= Q_BLOCK), 0.0, MASK_VALUE)

    lane = lax.broadcasted_iota(jnp.int32, (Q_BLOCK, LANES), 1)
    head_a = lane < HEAD_DIM
    stat_head = lane // STAT_LANES_PER_HEAD
    stat_is_max = (lane % STAT_LANES_PER_HEAD) < STAT_LANES_PER_HEAD // 2
    ones_cols = jnp.ones((keys, LANES), BF16)

    n_residues = q_ref.shape[2] // ATTN_WIDTH
    for res, j in [(res, j) for res in range(n_residues) for j in range(tq // Q_BLOCK)]:
        r0 = j * Q_BLOCK
        bias = bias_ref[(tile == 0).astype(jnp.int32)] if j == 0 else bias_ref[0]
        stat_tile = jnp.zeros((Q_BLOCK, LANES), F32)
        for hp in range(N_HEAD_PAIRS):
            cols = slice(res * ATTN_WIDTH + hp * LANES, res * ATTN_WIDTH + (hp + 1) * LANES)
            q2 = q_ref[0, r0:r0 + Q_BLOCK, cols]
            zero = jnp.zeros_like(q2)
            qs = jnp.concatenate([jnp.where(head_a, q2, zero), jnp.where(head_a, zero, q2)], axis=0)
            k2 = band(kp_ref, kc_ref, j, cols)
            v2 = band(vp_ref, vc_ref, j, cols)
            s = lax.dot_general(qs, k2, (((1,), (1,)), ((), ())), preferred_element_type=F32) + bias
            m = jnp.max(s, axis=1, keepdims=True)
            p = jnp.exp2(s - m).astype(BF16)
            ov = jnp.dot(p, jnp.concatenate([v2, ones_cols], axis=1), preferred_element_type=F32)
            o_ref[0, r0:r0 + Q_BLOCK, cols] = jnp.where(
                head_a, ov[:Q_BLOCK, :LANES], ov[Q_BLOCK:, :LANES]).astype(BF16)
            m_lanes = jnp.broadcast_to(m, (rows2, LANES))
            for half in range(HEADS_PER_VREG):
                rows = slice(half * Q_BLOCK, (half + 1) * Q_BLOCK)
                stat = jnp.where(stat_is_max, m_lanes[rows], ov[rows, LANES:])
                stat_tile = jnp.where(stat_head == HEADS_PER_VREG * hp + half, stat, stat_tile)
        stat_ref[0, r0:r0 + Q_BLOCK, res * LANES:(res + 1) * LANES] = stat_tile


def _attention_group(q3, k3, v3, dilation, queries_per_step):
    batch, sub_len, _ = q3.shape
    tq = min(queries_per_step, sub_len)
    n_res = queries_per_step // tq
    assert dilation % n_res == 0 and sub_len % tq == 0
    prev_blocks_per_tile = tq // Q_BLOCK
    cur = lambda b, r, n: (b, n, r)
    prev = lambda b, r, n: (b, jnp.maximum(n * prev_blocks_per_tile - 1, 0), r)
    return pl.pallas_call(
        _attn_kernel,
        out_shape=[jax.ShapeDtypeStruct(q3.shape, BF16),
                   jax.ShapeDtypeStruct((batch, sub_len, dilation * LANES), F32)],
        grid=(batch, dilation // n_res, sub_len // tq),
        in_specs=[
            pl.BlockSpec((1, tq, n_res * ATTN_WIDTH), cur),
            pl.BlockSpec((1, tq, n_res * ATTN_WIDTH), cur),
            pl.BlockSpec((1, Q_BLOCK, n_res * ATTN_WIDTH), prev),
            pl.BlockSpec((1, tq, n_res * ATTN_WIDTH), cur),
            pl.BlockSpec((1, Q_BLOCK, n_res * ATTN_WIDTH), prev),
        ],
        out_specs=[pl.BlockSpec((1, tq, n_res * ATTN_WIDTH), cur),
                   pl.BlockSpec((1, tq, n_res * LANES), cur)],
        scratch_shapes=[pltpu.VMEM((2, HEADS_PER_VREG * Q_BLOCK, 2 * Q_BLOCK), F32)],
        compiler_params=pltpu.CompilerParams(
            dimension_semantics=("parallel", "parallel", "arbitrary"),
            vmem_limit_bytes=VMEM_LIMIT_BYTES),
        name=f"attn_d{dilation}",
    )(q3, k3, k3, v3, v3)


def _mix_out_kernel(o1_ref, o2_ref, o3_ref, l1_ref, l2_ref, l3_ref, c_ref, ga_ref, gc_ref, x_ref,
                    wa_ref, wc_ref, wo_ref, g_ref, b_ref, out_ref, o_nat_ref, l_nat_ref, *, alpha, sub_rows):
    tm = x_ref.shape[0]
    n_slabs = ATTN_WIDTH // LANES
    for t0 in range(0, tm, sub_rows):
        rows_nat = slice(t0, t0 + sub_rows)

        def token_major(g, o_ref, l_ref, d):
            rows_view = slice(t0 // d, (t0 + sub_rows) // d)
            if d == 1:
                return o_ref[0, rows_view, :].astype(F32), l_ref[0, rows_view, :]
            for r in range(d):
                rows = pl.ds(t0 + r, sub_rows // d, stride=d)
                piece = o_ref[0, rows_view, r * ATTN_WIDTH:(r + 1) * ATTN_WIDTH].astype(F32)
                for c in range(n_slabs):
                    o_nat_ref[g, c, rows, :] = piece[:, c * LANES:(c + 1) * LANES]
                l_nat_ref[g, rows, :] = l_ref[0, rows_view, r * LANES:(r + 1) * LANES]
            return (jnp.concatenate([o_nat_ref[g, c, rows_nat, :] for c in range(n_slabs)], axis=1),
                    l_nat_ref[g, rows_nat, :])

        os_, stats = [], []
        for g, (o_ref, l_ref) in enumerate(((o1_ref, l1_ref), (o2_ref, l2_ref), (o3_ref, l3_ref))):
            o, stat = token_major(g, o_ref, l_ref, DILATED_GROUPS[g][1])
            os_.append(o)
            stats.append(stat)
        lane = lax.broadcasted_iota(jnp.int32, (sub_rows, LANES), 1)
        is_max_lane = (lane % STAT_LANES_PER_HEAD) < STAT_LANES_PER_HEAD // 2
        m = jnp.maximum(jnp.maximum(stats[0], stats[1]), stats[2])
        es = [jnp.exp2(stat - m) for stat in stats]
        dens = [pltpu.roll(stat, LANES - STAT_LANES_PER_HEAD // 2, 1) for stat in stats]
        inv = 1.0 / (es[0] * dens[0] + es[1] * dens[1] + es[2] * dens[2])
        src = lax.broadcasted_iota(jnp.int32, (2 * LANES, ATTN_WIDTH), 0) % LANES
        dst_head = lax.broadcasted_iota(jnp.int32, (2 * LANES, ATTN_WIDTH), 1) // HEAD_DIM
        spread = jnp.where(src == dst_head * STAT_LANES_PER_HEAD, 1.0, 0.0).astype(BF16)
        o_mix = jnp.zeros((sub_rows, ATTN_WIDTH), F32)
        for e, o in zip(es, os_):
            w = jnp.where(is_max_lane, e * inv, 0.0)
            w_hi = w.astype(BF16)
            w_lo = (w - w_hi.astype(F32)).astype(BF16)
            w_full = jnp.dot(jnp.concatenate([w_hi, w_lo], axis=1), spread, preferred_element_type=F32)
            o_mix = o_mix + w_full * o
        y_attn = jnp.dot(o_mix.astype(BF16), wa_ref[...], preferred_element_type=F32)
        y_conv = jnp.dot(c_ref[rows_nat, :], wc_ref[...], preferred_element_type=F32)
        merged = ga_ref[rows_nat, :].astype(F32) * y_attn + gc_ref[rows_nat, :].astype(F32) * y_conv
        mix = jnp.dot(merged.astype(BF16), wo_ref[...], preferred_element_type=F32)
        out_ref[rows_nat, :] = _layer_norm_rows(alpha * x_ref[rows_nat, :] + mix, g_ref[...], b_ref[...])


def _mix_out(os_, stats, c, ga, gc, x2, wa, wc, wo, ln_g, ln_b, alpha, seq, tm):
    tokens = x2.shape[0]
    n_seq_tiles = seq // tm
    row = lambda i: (i, 0)
    view = lambda i: (i // n_seq_tiles, i % n_seq_tiles, 0)
    const = lambda i: (0, 0)
    resident = lambda a: pl.BlockSpec(a.shape, const, pipeline_mode=pl.Buffered(1))
    o_specs = [pl.BlockSpec((1, tm // d, d * ATTN_WIDTH), view) for _, d in DILATED_GROUPS]
    l_specs = [pl.BlockSpec((1, tm // d, d * LANES), view) for _, d in DILATED_GROUPS]
    return pl.pallas_call(
        functools.partial(_mix_out_kernel, alpha=alpha, sub_rows=tm // 2),
        out_shape=jax.ShapeDtypeStruct((tokens, D_MODEL), F32),
        grid=(tokens // tm,),
        in_specs=(o_specs + l_specs
                  + [pl.BlockSpec((tm, CONV_CHANNELS), row)]
                  + [pl.BlockSpec((tm, D_MODEL), row)] * 3
                  + [resident(wa), resident(wc), resident(wo), resident(ln_g), resident(ln_b)]),
        out_specs=pl.BlockSpec((tm, D_MODEL), row),
        scratch_shapes=[pltpu.VMEM((N_GROUPS, ATTN_WIDTH // LANES, tm, LANES), F32),
                        pltpu.VMEM((N_GROUPS, tm, LANES), F32)],
        compiler_params=pltpu.CompilerParams(
            dimension_semantics=("parallel",), vmem_limit_bytes=VMEM_LIMIT_BYTES),
        name="mix_out",
    )(*os_, *stats, c, ga, gc, x2, wa, wc, wo, ln_g, ln_b)


def _ffn_kernel(x_ref, wg_ref, wu_ref, wd_ref, g_ref, b_ref, out_ref, h_ref, *, alpha, chunk, sub_rows):
    d_ff = wg_ref.shape[1]
    for t0 in range(0, x_ref.shape[0], sub_rows):
        rows = slice(t0, t0 + sub_rows)
        x = x_ref[rows, :]
        xb = x.astype(BF16)
        for c0 in range(0, d_ff, chunk):
            gate = jnp.dot(xb, wg_ref[:, c0:c0 + chunk], preferred_element_type=F32)
            up = jnp.dot(xb, wu_ref[:, c0:c0 + chunk], preferred_element_type=F32)
            h_ref[rows, c0:c0 + chunk] = (gate * jax.nn.sigmoid(gate) * up).astype(BF16)
        ff = jnp.dot(h_ref[rows, :], wd_ref[...], preferred_element_type=F32)
        out_ref[rows, :] = _layer_norm_rows(alpha * x + ff, g_ref[...], b_ref[...])


def _ffn(x1, wg, wu, wd, ln_g, ln_b, alpha, tm):
    tokens = x1.shape[0]
    d_ff = wg.shape[1]
    row = lambda i: (i, 0)
    const = lambda i: (0, 0)
    resident = lambda a: pl.BlockSpec(a.shape, const, pipeline_mode=pl.Buffered(1))
    return pl.pallas_call(
        functools.partial(_ffn_kernel, alpha=alpha, chunk=256, sub_rows=tm // 2),
        out_shape=jax.ShapeDtypeStruct((tokens, D_MODEL), F32),
        grid=(tokens // tm,),
        in_specs=[pl.BlockSpec((tm, D_MODEL), row),
                  resident(wg), resident(wu), resident(wd), resident(ln_g), resident(ln_b)],
        out_specs=pl.BlockSpec((tm, D_MODEL), row),
        scratch_shapes=[pltpu.VMEM((tm, d_ff), BF16)],
        compiler_params=pltpu.CompilerParams(
            dimension_semantics=("parallel",), vmem_limit_bytes=VMEM_LIMIT_BYTES),
        name="ffn",
    )(x1, wg, wu, wd, ln_g, ln_b)


def _rotary_tables(seq):
    inv_freq = 1.0 / (ROPE_THETA ** (jnp.arange(0, HEAD_DIM, 2, dtype=F32) / HEAD_DIM))
    hi = (jnp.arange(seq // ROPE_SPLIT, dtype=F32) * ROPE_SPLIT)[:, None] * inv_freq[None, :]
    lo = jnp.arange(ROPE_SPLIT, dtype=F32)[:, None] * inv_freq[None, :]
    cos_hi, sin_hi, cos_lo, sin_lo = lax.optimization_barrier(
        (jnp.cos(hi)[:, None, :], jnp.sin(hi)[:, None, :], jnp.cos(lo)[None], jnp.sin(lo)[None]))
    cos = (cos_hi * cos_lo - sin_hi * sin_lo).reshape(seq, HEAD_DIM // 2)
    sin = (sin_hi * cos_lo + cos_hi * sin_lo).reshape(seq, HEAD_DIM // 2)
    cos_t = jnp.concatenate([cos, cos] * HEADS_PER_VREG, axis=1)
    sin_t = jnp.concatenate([-sin, sin] * HEADS_PER_VREG, axis=1)
    return cos_t, sin_t


def kernel(x, w_in, conv_w, conv_b, conv_ln_g, conv_ln_b, w_attn_out, w_conv_out, w_o, ln1_g, ln1_b,
           w_ffn_gate, w_ffn_up, w_ffn_down, ln2_g, ln2_b):
    batch, seq, d_model = x.shape
    depth = w_in.shape[0]
    alpha = (2 * depth) ** 0.25
    tokens = batch * seq
    tm = 512
    cos_t, sin_t = _rotary_tables(seq)
    vec = lambda a: a.reshape(1, -1)
    h = x.reshape(tokens, d_model)
    for l in range(depth):
        outs = _in_proj(h, w_in[l].astype(BF16), cos_t, sin_t, batch, seq, tm)
        qkv, (u_even, u_odd, ga, gc) = outs[:3 * N_GROUPS], outs[3 * N_GROUPS:]
        conv_taps = jnp.repeat(conv_w[l].astype(BF16), PACKED_ROWS, axis=0)
        c = _conv_branch(u_even, u_odd, conv_taps, vec(conv_b[l]), vec(conv_ln_g[l]), vec(conv_ln_b[l]), tm)
        os_, stats = [], []
        for g, (_, dilation) in enumerate(DILATED_GROUPS):
            o, stat = _attention_group(qkv[3 * g], qkv[3 * g + 1], qkv[3 * g + 2], dilation,
                                       queries_per_step=4 * tm)
            os_.append(o)
            stats.append(stat)
        h1 = _mix_out(os_, stats, c, ga, gc, h, w_attn_out[l].astype(BF16), w_conv_out[l].astype(BF16),
                      w_o[l].astype(BF16), vec(ln1_g[l]), vec(ln1_b[l]), alpha, seq, 2 * tm)
        h = _ffn(h1, w_ffn_gate[l].astype(BF16), w_ffn_up[l].astype(BF16), w_ffn_down[l].astype(BF16),
                 vec(ln2_g[l]), vec(ln2_b[l]), alpha, 2 * tm)
    return h.reshape(batch, seq, d_model)
```

```python
import functools

import jax
import jax.numpy as jnp
from jax import lax
from jax.experimental import pallas as pl
from jax.experimental.pallas import tpu as pltpu

D_MODEL = 1024
HEAD_DIM = 64
N_HEADS = 8
ATTN_WIDTH = N_HEADS * HEAD_DIM
DILATED_GROUPS = ((128, 1), (512, 4), (2048, 16))
N_GROUPS = len(DILATED_GROUPS)
CONV_CHANNELS = 512
CONV_WIDTH = 31
ROPE_THETA = 10000.0
LN_EPS = 1e-5
QKV_COLS = N_GROUPS * 3 * ATTN_WIDTH
CONV_COL0 = QKV_COLS
GATE_COL0 = QKV_COLS + 2 * CONV_CHANNELS

LANES = 128
SUBLANES = 8
PACKED_ROWS = 2 * SUBLANES
HEADS_PER_VREG = LANES // HEAD_DIM
N_HEAD_PAIRS = N_HEADS // HEADS_PER_VREG
STAT_LANES_PER_HEAD = LANES // N_HEADS
LOG2_E = 1.4426950408889634
ROPE_SPLIT = 128
Q_BLOCK = 128
CONV_HALO = 32
MASK_VALUE = -1e30
VMEM_LIMIT_BYTES = 56 * 1024 * 1024
MXU_COLUMNS = 256

TOKEN_TILE = 512
WIDE_TOKEN_TILE = 2 * TOKEN_TILE
ATTN_QUERIES_PER_STEP = 8 * TOKEN_TILE

F32 = jnp.float32
BF16 = jnp.bfloat16

assert all(window // dilation == Q_BLOCK for window, dilation in DILATED_GROUPS)


def _layer_norm_rows(h, g, b):
    mu = jnp.mean(h, axis=-1, keepdims=True)
    d = h - mu
    var = jnp.mean(d * d, axis=-1, keepdims=True)
    return d * lax.rsqrt(var + LN_EPS) * g + b


def _in_proj_kernel(x_ref, xh_ref, w_ref, cos_ref, sin_ref, *rest, n_seq_tiles):
    out_refs, (slab_ref, ext_ref) = rest[:-2], rest[-2:]
    qkv_refs = out_refs[:3 * N_GROUPS]
    u_even_ref, u_odd_ref, ga_ref, gc_ref = out_refs[3 * N_GROUPS:]
    tm = x_ref.shape[0]
    n_slabs = x_ref.shape[1] // LANES
    xb = x_ref[...].astype(BF16)
    first_tile = (pl.program_id(0) % n_seq_tiles) == 0
    halo = jnp.where(first_tile, 0.0, xh_ref[...])
    for c in range(n_slabs):
        slab_ref[c, 0:CONV_HALO, :] = halo[:, c * LANES:(c + 1) * LANES]
        slab_ref[c, CONV_HALO:CONV_HALO + tm, :] = x_ref[:, c * LANES:(c + 1) * LANES]
    lane = lax.broadcasted_iota(jnp.int32, (tm, LANES), 1)
    first_half = (lane & (HEAD_DIM // 2)) == 0

    def residue_major(d):
        if d == 1:
            return xb
        rows = [jnp.concatenate([slab_ref[c, pl.ds(CONV_HALO + r, tm // d, stride=d), :]
                                 for c in range(n_slabs)], axis=1) for r in range(d)]
        return jnp.concatenate(rows, axis=0).astype(BF16)

    def table_rows(ref, d):
        if d == 1:
            return ref[...]
        return jnp.concatenate([ref[pl.ds(r, tm // d, stride=d), :] for r in range(d)], axis=0)

    def mm(lhs, col0, width):
        return jnp.dot(lhs, w_ref[:, col0:col0 + width], preferred_element_type=F32)

    window = CONV_HALO + tm
    pitch = window // SUBLANES
    x_win = jnp.concatenate(
        [jnp.concatenate([slab_ref[c, pl.ds(a, SUBLANES, stride=pitch), :] for c in range(n_slabs)], axis=1)
         for a in range(pitch)], axis=0).astype(BF16)
    u = mm(x_win, CONV_COL0, CONV_CHANNELS) * jax.nn.sigmoid(mm(x_win, CONV_COL0 + CONV_CHANNELS, CONV_CHANNELS))
    n_wrap = CONV_WIDTH - 1
    ext_ref[n_wrap * SUBLANES:, :] = u
    ext_ref[0:n_wrap * SUBLANES, :] = ext_ref[pitch * SUBLANES - 1:(pitch + n_wrap) * SUBLANES - 1, :]
    u_even_ref[0] = ext_ref[...].astype(BF16)
    u_odd_ref[0] = ext_ref[SUBLANES:SUBLANES + u_odd_ref.shape[1], :].astype(BF16)

    def rope(h, cos_t, sin_t):
        outs = []
        for c in range(h.shape[1] // LANES):
            hs = h[:, c * LANES:(c + 1) * LANES]
            partner = jnp.where(first_half,
                                pltpu.roll(hs, LANES - HEAD_DIM // 2, 1),
                                pltpu.roll(hs, HEAD_DIM // 2, 1))
            outs.append(hs * cos_t + partner * sin_t)
        return jnp.concatenate(outs, axis=1)

    def store_residue_view(ref, h, d):
        rows = tm // d
        for r in range(d):
            ref[0, :, r * ATTN_WIDTH:(r + 1) * ATTN_WIDTH] = h[r * rows:(r + 1) * rows, :].astype(BF16)

    for g, (_, d) in enumerate(DILATED_GROUPS):
        base = g * 3 * ATTN_WIDTH
        xg = residue_major(d)
        cos_t = table_rows(cos_ref, d)
        sin_t = table_rows(sin_ref, d)
        q = rope(mm(xg, base, ATTN_WIDTH), cos_t, sin_t) * (LOG2_E * HEAD_DIM ** -0.5)
        store_residue_view(qkv_refs[3 * g], q, d)
        store_residue_view(qkv_refs[3 * g + 1], rope(mm(xg, base + ATTN_WIDTH, ATTN_WIDTH), cos_t, sin_t), d)
        store_residue_view(qkv_refs[3 * g + 2], mm(xg, base + 2 * ATTN_WIDTH, ATTN_WIDTH), d)

    ga_ref[...] = jax.nn.sigmoid(mm(xb, GATE_COL0, D_MODEL)).astype(BF16)
    gc_ref[...] = jax.nn.sigmoid(mm(xb, GATE_COL0 + D_MODEL, D_MODEL)).astype(BF16)


def _conv_window_rows(tm):
    return (CONV_WIDTH - 1) * SUBLANES + CONV_HALO + tm


def _in_proj(x2, w_in_b, cos_t, sin_t, batch, seq, tm):
    tokens = x2.shape[0]
    n_tiles = tokens // tm
    n_seq_tiles = seq // tm
    halo_blocks_per_tile = tm // CONV_HALO
    window = CONV_HALO + tm
    assert window % SUBLANES == 0 and (window // SUBLANES) % 8 != 0
    ext_rows = _conv_window_rows(tm)
    row = lambda i: (i, 0)
    const = lambda i: (0, 0)
    tab = lambda i: (i % n_seq_tiles, 0)
    tile3 = lambda i: (i, 0, 0)
    view = lambda i: (i // n_seq_tiles, i % n_seq_tiles, 0)
    qkv_shapes, qkv_specs = [], []
    for _, d in DILATED_GROUPS:
        qkv_shapes += [jax.ShapeDtypeStruct((batch, seq // d, d * ATTN_WIDTH), BF16)] * 3
        qkv_specs += [pl.BlockSpec((1, tm // d, d * ATTN_WIDTH), view)] * 3
    u_rows = (ext_rows, ext_rows - PACKED_ROWS)
    out_shape = (qkv_shapes
                 + [jax.ShapeDtypeStruct((n_tiles, r, CONV_CHANNELS), BF16) for r in u_rows]
                 + [jax.ShapeDtypeStruct((tokens, D_MODEL), BF16)] * 2)
    out_specs = (qkv_specs
                 + [pl.BlockSpec((1, r, CONV_CHANNELS), tile3) for r in u_rows]
                 + [pl.BlockSpec((tm, D_MODEL), row)] * 2)
    return pl.pallas_call(
        functools.partial(_in_proj_kernel, n_seq_tiles=n_seq_tiles),
        out_shape=out_shape,
        grid=(n_tiles,),
        in_specs=[
            pl.BlockSpec((tm, D_MODEL), row),
            pl.BlockSpec((CONV_HALO, D_MODEL), lambda i: (jnp.maximum(i * halo_blocks_per_tile - 1, 0), 0)),
            pl.BlockSpec(w_in_b.shape, const, pipeline_mode=pl.Buffered(1)),
            pl.BlockSpec((tm, LANES), tab),
            pl.BlockSpec((tm, LANES), tab),
        ],
        out_specs=out_specs,
        scratch_shapes=[pltpu.VMEM((D_MODEL // LANES, window, LANES), F32),
                        pltpu.VMEM((ext_rows, CONV_CHANNELS), F32)],
        compiler_params=pltpu.CompilerParams(
            dimension_semantics=("parallel",), vmem_limit_bytes=VMEM_LIMIT_BYTES),
        name="in_proj",
    )(x2, x2, w_in_b, cos_t, sin_t)


def _conv_kernel(u_even_ref, u_odd_ref, cw_ref, cb_ref, cg_ref, cbeta_ref, c_ref, acc_ref, cnat_ref):
    tm = c_ref.shape[0]
    window = CONV_HALO + tm
    pitch = window // SUBLANES
    u_refs = (u_even_ref, u_odd_ref)
    n_blocks = pitch // 2
    accs = [None] * n_blocks
    for j in range(CONV_WIDTH):
        for b in range(n_blocks):
            t = (2 * b + j) // 2
            term = (u_refs[j % 2][0, t * PACKED_ROWS:(t + 1) * PACKED_ROWS, :].astype(F32)
                    * cw_ref[j * PACKED_ROWS:(j + 1) * PACKED_ROWS, :].astype(F32))
            accs[b] = term if accs[b] is None else accs[b] + term
    for b in range(n_blocks):
        acc_ref[b * PACKED_ROWS:(b + 1) * PACKED_ROWS, :] = accs[b]
    rows_per_chunk = 4 * SUBLANES
    for r0 in range(0, window, rows_per_chunk):
        y = _layer_norm_rows(acc_ref[r0:r0 + rows_per_chunk, :] + cb_ref[...], cg_ref[...], cbeta_ref[...])
        y = y * jax.nn.sigmoid(y)
        for i in range(rows_per_chunk // SUBLANES):
            for c in range(CONV_CHANNELS // LANES):
                cnat_ref[c, pl.ds(r0 // SUBLANES + i, SUBLANES, stride=pitch), :] = (
                    y[i * SUBLANES:(i + 1) * SUBLANES, c * LANES:(c + 1) * LANES])
    c_ref[...] = jnp.concatenate([cnat_ref[c, CONV_HALO:CONV_HALO + tm, :]
                                  for c in range(CONV_CHANNELS // LANES)], axis=1).astype(BF16)


def _conv_branch(u_even, u_odd, conv_taps, conv_b, ln_g, ln_b, tm):
    n_tiles = u_even.shape[0]
    window = CONV_HALO + tm
    const = lambda i: (0, 0)
    tile3 = lambda i: (i, 0, 0)
    vec = pl.BlockSpec((1, CONV_CHANNELS), const)
    return pl.pallas_call(
        _conv_kernel,
        out_shape=jax.ShapeDtypeStruct((n_tiles * tm, CONV_CHANNELS), BF16),
        grid=(n_tiles,),
        in_specs=[pl.BlockSpec((1,) + u_even.shape[1:], tile3),
                  pl.BlockSpec((1,) + u_odd.shape[1:], tile3),
                  pl.BlockSpec(conv_taps.shape, const),
                  vec, vec, vec],
        out_specs=pl.BlockSpec((tm, CONV_CHANNELS), lambda i: (i, 0)),
        scratch_shapes=[pltpu.VMEM((window, CONV_CHANNELS), F32),
                        pltpu.VMEM((CONV_CHANNELS // LANES, window, LANES), F32)],
        compiler_params=pltpu.CompilerParams(
            dimension_semantics=("parallel",), vmem_limit_bytes=VMEM_LIMIT_BYTES),
        name="conv_branch",
    )(u_even, u_odd, conv_taps, conv_b, ln_g, ln_b)


def _attn_kernel(q_ref, kc_ref, kp_ref, vc_ref, vp_ref, o_ref, stat_ref, bias_ref):
    tq = q_ref.shape[1]
    tile = pl.program_id(2)

    def band(prev_ref, cur_ref, j, cols):
        if j == 0:
            return jnp.concatenate([prev_ref[0, :, cols], cur_ref[0, 0:Q_BLOCK, cols]], axis=0)
        return cur_ref[0, (j - 1) * Q_BLOCK:(j + 1) * Q_BLOCK, cols]

    rows2 = HEADS_PER_VREG * Q_BLOCK
    keys = 2 * Q_BLOCK
    qi = lax.broadcasted_iota(jnp.int32, (rows2, keys), 0) & (Q_BLOCK - 1)
    kk = lax.broadcasted_iota(jnp.int32, (rows2, keys), 1)
    dist = qi + Q_BLOCK - kk
    in_window = (dist >= 0) & (dist <= Q_BLOCK)
    bias_ref[0] = jnp.where(in_window, 0.0, MASK_VALUE)
    bias_ref[1] = jnp.where(in_window & (kk >= Q_BLOCK), 0.0, MASK_VALUE)

    lane = lax.broadcasted_iota(jnp.int32, (Q_BLOCK, LANES), 1)
    head_a = lane < HEAD_DIM
    stat_head = lane // STAT_LANES_PER_HEAD
    stat_is_max = (lane % STAT_LANES_PER_HEAD) < STAT_LANES_PER_HEAD // 2
    ones_cols = jnp.ones((keys, LANES), BF16)

    n_residues = q_ref.shape[2] // ATTN_WIDTH
    for res, j in [(res, j) for res in range(n_residues) for j in range(tq // Q_BLOCK)]:
        r0 = j * Q_BLOCK
        bias = bias_ref[(tile == 0).astype(jnp.int32)] if j == 0 else bias_ref[0]
        stat_tile = jnp.zeros((Q_BLOCK, LANES), F32)
        for hp in range(N_HEAD_PAIRS):
            cols = slice(res * ATTN_WIDTH + hp * LANES, res * ATTN_WIDTH + (hp + 1) * LANES)
            q2 = q_ref[0, r0:r0 + Q_BLOCK, cols]
            zero = jnp.zeros_like(q2)
            qs = jnp.concatenate([jnp.where(head_a, q2, zero), jnp.where(head_a, zero, q2)], axis=0)
            k2 = band(kp_ref, kc_ref, j, cols)
            v2 = band(vp_ref, vc_ref, j, cols)
            s = lax.dot_general(qs, k2, (((1,), (1,)), ((), ())), preferred_element_type=F32) + bias
            m = jnp.max(s, axis=1, keepdims=True)
            p = jnp.exp2(s - m).astype(BF16)
            ov = jnp.dot(p, jnp.concatenate([v2, ones_cols], axis=1), preferred_element_type=F32)
            o_ref[0, r0:r0 + Q_BLOCK, cols] = jnp.where(
                head_a, ov[:Q_BLOCK, :LANES], ov[Q_BLOCK:, :LANES]).astype(BF16)
            m_lanes = jnp.broadcast_to(m, (rows2, LANES))
            for half in range(HEADS_PER_VREG):
                rows = slice(half * Q_BLOCK, (half + 1) * Q_BLOCK)
                stat = jnp.where(stat_is_max, m_lanes[rows], ov[rows, LANES:])
                stat_tile = jnp.where(stat_head == HEADS_PER_VREG * hp + half, stat, stat_tile)
        stat_ref[0, r0:r0 + Q_BLOCK, res * LANES:(res + 1) * LANES] = stat_tile


def _attention_group(q3, k3, v3, dilation, queries_per_step):
    batch, sub_len, _ = q3.shape
    tq = min(queries_per_step, sub_len)
    n_res = queries_per_step // tq
    assert dilation % n_res == 0 and sub_len % tq == 0
    prev_blocks_per_tile = tq // Q_BLOCK
    cur = lambda b, r, n: (b, n, r)
    prev = lambda b, r, n: (b, jnp.maximum(n * prev_blocks_per_tile - 1, 0), r)
    return pl.pallas_call(
        _attn_kernel,
        out_shape=[jax.ShapeDtypeStruct(q3.shape, BF16),
                   jax.ShapeDtypeStruct((batch, sub_len, dilation * LANES), F32)],
        grid=(batch, dilation // n_res, sub_len // tq),
        in_specs=[
            pl.BlockSpec((1, tq, n_res * ATTN_WIDTH), cur),
            pl.BlockSpec((1, tq, n_res * ATTN_WIDTH), cur),
            pl.BlockSpec((1, Q_BLOCK, n_res * ATTN_WIDTH), prev),
            pl.BlockSpec((1, tq, n_res * ATTN_WIDTH), cur),
            pl.BlockSpec((1, Q_BLOCK, n_res * ATTN_WIDTH), prev),
        ],
        out_specs=[pl.BlockSpec((1, tq, n_res * ATTN_WIDTH), cur),
                   pl.BlockSpec((1, tq, n_res * LANES), cur)],
        scratch_shapes=[pltpu.VMEM((2, HEADS_PER_VREG * Q_BLOCK, 2 * Q_BLOCK), F32)],
        compiler_params=pltpu.CompilerParams(
            dimension_semantics=("parallel", "parallel", "arbitrary"),
            vmem_limit_bytes=VMEM_LIMIT_BYTES),
        name=f"attn_d{dilation}",
    )(q3, k3, k3, v3, v3)


def _mix_out_kernel(o1_ref, o2_ref, o3_ref, l1_ref, l2_ref, l3_ref, c_ref, ga_ref, gc_ref, x_ref,
                    wa_ref, wc_ref, wo_ref, g_ref, b_ref, out_ref, o_nat_ref, l_nat_ref, *, alpha, sub_rows):
    tm = x_ref.shape[0]
    n_slabs = ATTN_WIDTH // LANES
    for t0 in range(0, tm, sub_rows):
        rows_nat = slice(t0, t0 + sub_rows)

        def token_major(g, o_ref, l_ref, d):
            rows_view = slice(t0 // d, (t0 + sub_rows) // d)
            if d == 1:
                return o_ref[0, rows_view, :].astype(F32), l_ref[0, rows_view, :]
            for r in range(d):
                rows = pl.ds(t0 + r, sub_rows // d, stride=d)
                piece = o_ref[0, rows_view, r * ATTN_WIDTH:(r + 1) * ATTN_WIDTH].astype(F32)
                for c in range(n_slabs):
                    o_nat_ref[g, c, rows, :] = piece[:, c * LANES:(c + 1) * LANES]
                l_nat_ref[g, rows, :] = l_ref[0, rows_view, r * LANES:(r + 1) * LANES]
            return (jnp.concatenate([o_nat_ref[g, c, rows_nat, :] for c in range(n_slabs)], axis=1),
                    l_nat_ref[g, rows_nat, :])

        os_, stats = [], []
        for g, (o_ref, l_ref) in enumerate(((o1_ref, l1_ref), (o2_ref, l2_ref), (o3_ref, l3_ref))):
            o, stat = token_major(g, o_ref, l_ref, DILATED_GROUPS[g][1])
            os_.append(o)
            stats.append(stat)
        lane = lax.broadcasted_iota(jnp.int32, (sub_rows, LANES), 1)
        is_max_lane = (lane % STAT_LANES_PER_HEAD) < STAT_LANES_PER_HEAD // 2
        m = jnp.maximum(jnp.maximum(stats[0], stats[1]), stats[2])
        es = [jnp.exp2(stat - m) for stat in stats]
        dens = [pltpu.roll(stat, LANES - STAT_LANES_PER_HEAD // 2, 1) for stat in stats]
        inv = 1.0 / (es[0] * dens[0] + es[1] * dens[1] + es[2] * dens[2])
        src = lax.broadcasted_iota(jnp.int32, (2 * LANES, ATTN_WIDTH), 0) % LANES
        dst_head = lax.broadcasted_iota(jnp.int32, (2 * LANES, ATTN_WIDTH), 1) // HEAD_DIM
        spread = jnp.where(src == dst_head * STAT_LANES_PER_HEAD, 1.0, 0.0).astype(BF16)
        o_mix = jnp.zeros((sub_rows, ATTN_WIDTH), F32)
        for e, o in zip(es, os_):
            w = jnp.where(is_max_lane, e * inv, 0.0)
            w_hi = w.astype(BF16)
            w_lo = (w - w_hi.astype(F32)).astype(BF16)
            w_full = jnp.dot(jnp.concatenate([w_hi, w_lo], axis=1), spread, preferred_element_type=F32)
            o_mix = o_mix + w_full * o
        y_attn = jnp.dot(o_mix.astype(BF16), wa_ref[...], preferred_element_type=F32)
        y_conv = jnp.dot(c_ref[rows_nat, :], wc_ref[...], preferred_element_type=F32)
        merged = ga_ref[rows_nat, :].astype(F32) * y_attn + gc_ref[rows_nat, :].astype(F32) * y_conv
        mix = jnp.dot(merged.astype(BF16), wo_ref[...], preferred_element_type=F32)
        out_ref[rows_nat, :] = _layer_norm_rows(alpha * x_ref[rows_nat, :] + mix, g_ref[...], b_ref[...])


def _mix_out(os_, stats, c, ga, gc, x2, wa, wc, wo, ln_g, ln_b, alpha, seq, tm):
    tokens = x2.shape[0]
    n_seq_tiles = seq // tm
    row = lambda i: (i, 0)
    view = lambda i: (i // n_seq_tiles, i % n_seq_tiles, 0)
    const = lambda i: (0, 0)
    resident = lambda a: pl.BlockSpec(a.shape, const, pipeline_mode=pl.Buffered(1))
    o_specs = [pl.BlockSpec((1, tm // d, d * ATTN_WIDTH), view) for _, d in DILATED_GROUPS]
    l_specs = [pl.BlockSpec((1, tm // d, d * LANES), view) for _, d in DILATED_GROUPS]
    return pl.pallas_call(
        functools.partial(_mix_out_kernel, alpha=alpha, sub_rows=tm // 2),
        out_shape=jax.ShapeDtypeStruct((tokens, D_MODEL), F32),
        grid=(tokens // tm,),
        in_specs=(o_specs + l_specs
                  + [pl.BlockSpec((tm, CONV_CHANNELS), row)]
                  + [pl.BlockSpec((tm, D_MODEL), row)] * 3
                  + [resident(wa), resident(wc), resident(wo), resident(ln_g), resident(ln_b)]),
        out_specs=pl.BlockSpec((tm, D_MODEL), row),
        scratch_shapes=[pltpu.VMEM((N_GROUPS, ATTN_WIDTH // LANES, tm, LANES), F32),
                        pltpu.VMEM((N_GROUPS, tm, LANES), F32)],
        compiler_params=pltpu.CompilerParams(
            dimension_semantics=("parallel",), vmem_limit_bytes=VMEM_LIMIT_BYTES),
        name="mix_out",
    )(*os_, *stats, c, ga, gc, x2, wa, wc, wo, ln_g, ln_b)


def _ffn_kernel(x_ref, wg_ref, wu_ref, wd_ref, g_ref, b_ref, out_ref, h_ref, *, alpha, chunk, sub_rows):
    d_ff = wg_ref.shape[1]
    for t0 in range(0, x_ref.shape[0], sub_rows):
        rows = slice(t0, t0 + sub_rows)
        x = x_ref[rows, :]
        xb = x.astype(BF16)
        for c0 in range(0, d_ff, chunk):
            gate = jnp.dot(xb, wg_ref[:, c0:c0 + chunk], preferred_element_type=F32)
            up = jnp.dot(xb, wu_ref[:, c0:c0 + chunk], preferred_element_type=F32)
            h_ref[rows, c0:c0 + chunk] = (gate * jax.nn.sigmoid(gate) * up).astype(BF16)
        ff = jnp.dot(h_ref[rows, :], wd_ref[...], preferred_element_type=F32)
        out_ref[rows, :] = _layer_norm_rows(alpha * x + ff, g_ref[...], b_ref[...])


def _ffn(x1, wg, wu, wd, ln_g, ln_b, alpha, tm):
    tokens = x1.shape[0]
    d_ff = wg.shape[1]
    row = lambda i: (i, 0)
    const = lambda i: (0, 0)
    resident = lambda a: pl.BlockSpec(a.shape, const, pipeline_mode=pl.Buffered(1))
    return pl.pallas_call(
        functools.partial(_ffn_kernel, alpha=alpha, chunk=MXU_COLUMNS, sub_rows=tm // 2),
        out_shape=jax.ShapeDtypeStruct((tokens, D_MODEL), F32),
        grid=(tokens // tm,),
        in_specs=[pl.BlockSpec((tm, D_MODEL), row),
                  resident(wg), resident(wu), resident(wd), resident(ln_g), resident(ln_b)],
        out_specs=pl.BlockSpec((tm, D_MODEL), row),
        scratch_shapes=[pltpu.VMEM((tm, d_ff), BF16)],
        compiler_params=pltpu.CompilerParams(
            dimension_semantics=("parallel",), vmem_limit_bytes=VMEM_LIMIT_BYTES),
        name="ffn",
    )(x1, wg, wu, wd, ln_g, ln_b)


def _rotary_tables(seq):
    inv_freq = 1.0 / (ROPE_THETA ** (jnp.arange(0, HEAD_DIM, 2, dtype=F32) / HEAD_DIM))
    hi = (jnp.arange(seq // ROPE_SPLIT, dtype=F32) * ROPE_SPLIT)[:, None] * inv_freq[None, :]
    lo = jnp.arange(ROPE_SPLIT, dtype=F32)[:, None] * inv_freq[None, :]
    cos_hi, sin_hi, cos_lo, sin_lo = lax.optimization_barrier(
        (jnp.cos(hi)[:, None, :], jnp.sin(hi)[:, None, :], jnp.cos(lo)[None], jnp.sin(lo)[None]))
    cos = (cos_hi * cos_lo - sin_hi * sin_lo).reshape(seq, HEAD_DIM // 2)
    sin = (sin_hi * cos_lo + cos_hi * sin_lo).reshape(seq, HEAD_DIM // 2)
    cos_t = jnp.concatenate([cos, cos] * HEADS_PER_VREG, axis=1)
    sin_t = jnp.concatenate([-sin, sin] * HEADS_PER_VREG, axis=1)
    return cos_t, sin_t


def kernel(x, w_in, conv_w, conv_b, conv_ln_g, conv_ln_b, w_attn_out, w_conv_out, w_o, ln1_g, ln1_b,
           w_ffn_gate, w_ffn_up, w_ffn_down, ln2_g, ln2_b):
    batch, seq, d_model = x.shape
    depth = w_in.shape[0]
    alpha = (2 * depth) ** 0.25
    tokens = batch * seq
    assert (d_model == D_MODEL and seq % ATTN_QUERIES_PER_STEP == 0
            and (seq // max(d for _, d in DILATED_GROUPS)) % Q_BLOCK == 0)
    cos_t, sin_t = _rotary_tables(seq)
    vec = lambda a: a.reshape(1, -1)
    h = x.reshape(tokens, d_model)
    for l in range(depth):
        outs = _in_proj(h, w_in[l].astype(BF16), cos_t, sin_t, batch, seq, TOKEN_TILE)
        qkv, (u_even, u_odd, ga, gc) = outs[:3 * N_GROUPS], outs[3 * N_GROUPS:]
        conv_taps = jnp.repeat(conv_w[l].astype(BF16), PACKED_ROWS, axis=0)
        c = _conv_branch(u_even, u_odd, conv_taps, vec(conv_b[l]), vec(conv_ln_g[l]), vec(conv_ln_b[l]),
                         TOKEN_TILE)
        os_, stats = [], []
        for g, (_, dilation) in enumerate(DILATED_GROUPS):
            o, stat = _attention_group(qkv[3 * g], qkv[3 * g + 1], qkv[3 * g + 2], dilation,
                                       queries_per_step=ATTN_QUERIES_PER_STEP)
            os_.append(o)
            stats.append(stat)
        h1 = _mix_out(os_, stats, c, ga, gc, h, w_attn_out[l].astype(BF16), w_conv_out[l].astype(BF16),
                      w_o[l].astype(BF16), vec(ln1_g[l]), vec(ln1_b[l]), alpha, seq, WIDE_TOKEN_TILE)
        h = _ffn(h1, w_ffn_gate[l].astype(BF16), w_ffn_up[l].astype(BF16), w_ffn_down[l].astype(BF16),
                 vec(ln2_g[l]), vec(ln2_b[l]), alpha, WIDE_TOKEN_TILE)
    return h.reshape(batch, seq, d_model)
```

```python
import functools

import jax
import jax.numpy as jnp
from jax import lax
from jax.experimental import pallas as pl
from jax.experimental.pallas import tpu as pltpu

D_MODEL = 1024
HEAD_DIM = 64
N_HEADS = 8
ATTN_WIDTH = N_HEADS * HEAD_DIM
DILATED_GROUPS = ((128, 1), (512, 4), (2048, 16))
N_GROUPS = len(DILATED_GROUPS)
CONV_CHANNELS = 512
CONV_WIDTH = 31
ROPE_THETA = 10000.0
LN_EPS = 1e-5
QKV_COLS = N_GROUPS * 3 * ATTN_WIDTH
CONV_COL0 = QKV_COLS
GATE_COL0 = QKV_COLS + 2 * CONV_CHANNELS

LANES = 128
SUBLANES = 8
PACKED_ROWS = 2 * SUBLANES
HEADS_PER_VREG = LANES // HEAD_DIM
N_HEAD_PAIRS = N_HEADS // HEADS_PER_VREG
STAT_LANES_PER_HEAD = LANES // N_HEADS
LOG2_E = 1.4426950408889634
ROPE_SPLIT = 128
Q_BLOCK = 128
CONV_HALO = 32
MASK_VALUE = -1e30
VMEM_LIMIT_BYTES = 56 * 1024 * 1024
MXU_COLUMNS = 256

TOKEN_TILE = 512
WIDE_TOKEN_TILE = 2 * TOKEN_TILE
ATTN_QUERIES_PER_STEP = 4 * TOKEN_TILE

F32 = jnp.float32
BF16 = jnp.bfloat16

assert all(window // dilation == Q_BLOCK for window, dilation in DILATED_GROUPS)


def _layer_norm_rows(h, g, b):
    mu = jnp.mean(h, axis=-1, keepdims=True)
    d = h - mu
    var = jnp.mean(d * d, axis=-1, keepdims=True)
    return d * lax.rsqrt(var + LN_EPS) * g + b


def _in_proj_kernel(x_ref, xh_ref, w_ref, cos_ref, sin_ref, *rest, n_seq_tiles):
    out_refs, (slab_ref, ext_ref) = rest[:-2], rest[-2:]
    qkv_refs = out_refs[:3 * N_GROUPS]
    u_even_ref, u_odd_ref, ga_ref, gc_ref = out_refs[3 * N_GROUPS:]
    tm = x_ref.shape[0]
    n_slabs = x_ref.shape[1] // LANES
    xb = x_ref[...].astype(BF16)
    first_tile = (pl.program_id(0) % n_seq_tiles) == 0
    halo = jnp.where(first_tile, 0.0, xh_ref[...])
    for c in range(n_slabs):
        slab_ref[c, 0:CONV_HALO, :] = halo[:, c * LANES:(c + 1) * LANES]
        slab_ref[c, CONV_HALO:CONV_HALO + tm, :] = x_ref[:, c * LANES:(c + 1) * LANES]
    lane = lax.broadcasted_iota(jnp.int32, (tm, LANES), 1)
    first_half = (lane & (HEAD_DIM // 2)) == 0

    def residue_major(d):
        if d == 1:
            return xb
        rows = [jnp.concatenate([slab_ref[c, pl.ds(CONV_HALO + r, tm // d, stride=d), :]
                                 for c in range(n_slabs)], axis=1) for r in range(d)]
        return jnp.concatenate(rows, axis=0).astype(BF16)

    def table_rows(ref, d):
        if d == 1:
            return ref[...]
        return jnp.concatenate([ref[pl.ds(r, tm // d, stride=d), :] for r in range(d)], axis=0)

    def mm(lhs, col0, width):
        return jnp.dot(lhs, w_ref[:, col0:col0 + width], preferred_element_type=F32)

    window = CONV_HALO + tm
    pitch = window // SUBLANES
    x_win = jnp.concatenate(
        [jnp.concatenate([slab_ref[c, pl.ds(a, SUBLANES, stride=pitch), :] for c in range(n_slabs)], axis=1)
         for a in range(pitch)], axis=0).astype(BF16)
    u = mm(x_win, CONV_COL0, CONV_CHANNELS) * jax.nn.sigmoid(mm(x_win, CONV_COL0 + CONV_CHANNELS, CONV_CHANNELS))
    n_wrap = CONV_WIDTH - 1
    ext_ref[n_wrap * SUBLANES:, :] = u
    ext_ref[0:n_wrap * SUBLANES, :] = ext_ref[pitch * SUBLANES - 1:(pitch + n_wrap) * SUBLANES - 1, :]
    u_even_ref[0] = ext_ref[...].astype(BF16)
    u_odd_ref[0] = ext_ref[SUBLANES:SUBLANES + u_odd_ref.shape[1], :].astype(BF16)

    def rope(h, cos_t, sin_t):
        outs = []
        for c in range(h.shape[1] // LANES):
            hs = h[:, c * LANES:(c + 1) * LANES]
            partner = jnp.where(first_half,
                                pltpu.roll(hs, LANES - HEAD_DIM // 2, 1),
                                pltpu.roll(hs, HEAD_DIM // 2, 1))
            outs.append(hs * cos_t + partner * sin_t)
        return jnp.concatenate(outs, axis=1)

    def store_residue_view(ref, h, d):
        rows = tm // d
        for r in range(d):
            ref[0, :, r * ATTN_WIDTH:(r + 1) * ATTN_WIDTH] = h[r * rows:(r + 1) * rows, :].astype(BF16)

    for g, (_, d) in enumerate(DILATED_GROUPS):
        base = g * 3 * ATTN_WIDTH
        xg = residue_major(d)
        cos_t = table_rows(cos_ref, d)
        sin_t = table_rows(sin_ref, d)
        q = rope(mm(xg, base, ATTN_WIDTH), cos_t, sin_t) * (LOG2_E * HEAD_DIM ** -0.5)
        store_residue_view(qkv_refs[3 * g], q, d)
        store_residue_view(qkv_refs[3 * g + 1], rope(mm(xg, base + ATTN_WIDTH, ATTN_WIDTH), cos_t, sin_t), d)
        store_residue_view(qkv_refs[3 * g + 2], mm(xg, base + 2 * ATTN_WIDTH, ATTN_WIDTH), d)

    ga_ref[...] = jax.nn.sigmoid(mm(xb, GATE_COL0, D_MODEL)).astype(BF16)
    gc_ref[...] = jax.nn.sigmoid(mm(xb, GATE_COL0 + D_MODEL, D_MODEL)).astype(BF16)


def _conv_window_rows(tm):
    return (CONV_WIDTH - 1) * SUBLANES + CONV_HALO + tm


def _in_proj(x2, w_in_b, cos_t, sin_t, batch, seq, tm):
    tokens = x2.shape[0]
    n_tiles = tokens // tm
    n_seq_tiles = seq // tm
    halo_blocks_per_tile = tm // CONV_HALO
    window = CONV_HALO + tm
    assert window % SUBLANES == 0 and (window // SUBLANES) % 8 != 0
    ext_rows = _conv_window_rows(tm)
    row = lambda i: (i, 0)
    const = lambda i: (0, 0)
    tab = lambda i: (i % n_seq_tiles, 0)
    tile3 = lambda i: (i, 0, 0)
    view = lambda i: (i // n_seq_tiles, i % n_seq_tiles, 0)
    qkv_shapes, qkv_specs = [], []
    for _, d in DILATED_GROUPS:
        qkv_shapes += [jax.ShapeDtypeStruct((batch, seq // d, d * ATTN_WIDTH), BF16)] * 3
        qkv_specs += [pl.BlockSpec((1, tm // d, d * ATTN_WIDTH), view)] * 3
    u_rows = (ext_rows, ext_rows - PACKED_ROWS)
    out_shape = (qkv_shapes
                 + [jax.ShapeDtypeStruct((n_tiles, r, CONV_CHANNELS), BF16) for r in u_rows]
                 + [jax.ShapeDtypeStruct((tokens, D_MODEL), BF16)] * 2)
    out_specs = (qkv_specs
                 + [pl.BlockSpec((1, r, CONV_CHANNELS), tile3) for r in u_rows]
                 + [pl.BlockSpec((tm, D_MODEL), row)] * 2)
    return pl.pallas_call(
        functools.partial(_in_proj_kernel, n_seq_tiles=n_seq_tiles),
        out_shape=out_shape,
        grid=(n_tiles,),
        in_specs=[
            pl.BlockSpec((tm, D_MODEL), row),
            pl.BlockSpec((CONV_HALO, D_MODEL), lambda i: (jnp.maximum(i * halo_blocks_per_tile - 1, 0), 0)),
            pl.BlockSpec(w_in_b.shape, const, pipeline_mode=pl.Buffered(1)),
            pl.BlockSpec((tm, LANES), tab),
            pl.BlockSpec((tm, LANES), tab),
        ],
        out_specs=out_specs,
        scratch_shapes=[pltpu.VMEM((D_MODEL // LANES, window, LANES), F32),
                        pltpu.VMEM((ext_rows, CONV_CHANNELS), F32)],
        compiler_params=pltpu.CompilerParams(
            dimension_semantics=("parallel",), vmem_limit_bytes=VMEM_LIMIT_BYTES),
        name="in_proj",
    )(x2, x2, w_in_b, cos_t, sin_t)


def _attn_kernel(q_ref, kc_ref, kp_ref, vc_ref, vp_ref, o_ref, stat_ref, bias_ref):
    tq = q_ref.shape[1]
    tile = pl.program_id(2)

    def band(prev_ref, cur_ref, j, cols):
        if j == 0:
            return jnp.concatenate([prev_ref[0, :, cols], cur_ref[0, 0:Q_BLOCK, cols]], axis=0)
        return cur_ref[0, (j - 1) * Q_BLOCK:(j + 1) * Q_BLOCK, cols]

    rows2 = HEADS_PER_VREG * Q_BLOCK
    keys = 2 * Q_BLOCK
    qi = lax.broadcasted_iota(jnp.int32, (rows2, keys), 0) & (Q_BLOCK - 1)
    kk = lax.broadcasted_iota(jnp.int32, (rows2, keys), 1)
    dist = qi + Q_BLOCK - kk
    in_window = (dist >= 0) & (dist <= Q_BLOCK)
    bias_ref[0] = jnp.where(in_window, 0.0, MASK_VALUE)
    bias_ref[1] = jnp.where(in_window & (kk >= Q_BLOCK), 0.0, MASK_VALUE)

    lane = lax.broadcasted_iota(jnp.int32, (Q_BLOCK, LANES), 1)
    head_a = lane < HEAD_DIM
    stat_head = lane // STAT_LANES_PER_HEAD
    stat_is_max = (lane % STAT_LANES_PER_HEAD) < STAT_LANES_PER_HEAD // 2
    ones_cols = jnp.ones((keys, LANES), BF16)

    n_residues = q_ref.shape[2] // ATTN_WIDTH
    for res, j in [(res, j) for res in range(n_residues) for j in range(tq // Q_BLOCK)]:
        r0 = j * Q_BLOCK
        bias = bias_ref[(tile == 0).astype(jnp.int32)] if j == 0 else bias_ref[0]
        stat_tile = jnp.zeros((Q_BLOCK, LANES), F32)
        for hp in range(N_HEAD_PAIRS):
            cols = slice(res * ATTN_WIDTH + hp * LANES, res * ATTN_WIDTH + (hp + 1) * LANES)
            q2 = q_ref[0, r0:r0 + Q_BLOCK, cols]
            zero = jnp.zeros_like(q2)
            qs = jnp.concatenate([jnp.where(head_a, q2, zero), jnp.where(head_a, zero, q2)], axis=0)
            k2 = band(kp_ref, kc_ref, j, cols)
            v2 = band(vp_ref, vc_ref, j, cols)
            s = lax.dot_general(qs, k2, (((1,), (1,)), ((), ())), preferred_element_type=F32) + bias
            m = jnp.max(s, axis=1, keepdims=True)
            p = jnp.exp2(s - m).astype(BF16)
            ov = jnp.dot(p, jnp.concatenate([v2, ones_cols], axis=1), preferred_element_type=F32)
            o_ref[0, r0:r0 + Q_BLOCK, cols] = jnp.where(
                head_a, ov[:Q_BLOCK, :LANES], ov[Q_BLOCK:, :LANES]).astype(BF16)
            m_lanes = jnp.broadcast_to(m, (rows2, LANES))
            for half in range(HEADS_PER_VREG):
                rows = slice(half * Q_BLOCK, (half + 1) * Q_BLOCK)
                stat = jnp.where(stat_is_max, m_lanes[rows], ov[rows, LANES:])
                stat_tile = jnp.where(stat_head == HEADS_PER_VREG * hp + half, stat, stat_tile)
        stat_ref[0, r0:r0 + Q_BLOCK, res * LANES:(res + 1) * LANES] = stat_tile


def _attention_group(q3, k3, v3, dilation, queries_per_step):
    batch, sub_len, _ = q3.shape
    tq = min(queries_per_step, sub_len)
    n_res = queries_per_step // tq
    assert dilation % n_res == 0 and sub_len % tq == 0
    prev_blocks_per_tile = tq // Q_BLOCK
    cur = lambda b, r, n: (b, n, r)
    prev = lambda b, r, n: (b, jnp.maximum(n * prev_blocks_per_tile - 1, 0), r)
    return pl.pallas_call(
        _attn_kernel,
        out_shape=[jax.ShapeDtypeStruct(q3.shape, BF16),
                   jax.ShapeDtypeStruct((batch, sub_len, dilation * LANES), F32)],
        grid=(batch, dilation // n_res, sub_len // tq),
        in_specs=[
            pl.BlockSpec((1, tq, n_res * ATTN_WIDTH), cur),
            pl.BlockSpec((1, tq, n_res * ATTN_WIDTH), cur),
            pl.BlockSpec((1, Q_BLOCK, n_res * ATTN_WIDTH), prev),
            pl.BlockSpec((1, tq, n_res * ATTN_WIDTH), cur),
            pl.BlockSpec((1, Q_BLOCK, n_res * ATTN_WIDTH), prev),
        ],
        out_specs=[pl.BlockSpec((1, tq, n_res * ATTN_WIDTH), cur),
                   pl.BlockSpec((1, tq, n_res * LANES), cur)],
        scratch_shapes=[pltpu.VMEM((2, HEADS_PER_VREG * Q_BLOCK, 2 * Q_BLOCK), F32)],
        compiler_params=pltpu.CompilerParams(
            dimension_semantics=("parallel", "parallel", "arbitrary"),
            vmem_limit_bytes=VMEM_LIMIT_BYTES),
        name=f"attn_d{dilation}",
    )(q3, k3, k3, v3, v3)


def _mix_out_kernel(o1_ref, o2_ref, o3_ref, l1_ref, l2_ref, l3_ref, u_even_ref, u_odd_ref, cw_ref, cb_ref,
                    cg_ref, cbeta_ref, ga_ref, gc_ref, x_ref, wa_ref, wc_ref, wo_ref, g_ref, b_ref, out_ref,
                    o_nat_ref, l_nat_ref, acc_ref, cnat_ref, *, alpha, sub_rows):
    tm = x_ref.shape[0]
    n_slabs = ATTN_WIDTH // LANES
    dilated_slots = [g for g, (_, d) in enumerate(DILATED_GROUPS) if d > 1]
    for t0 in range(0, tm, sub_rows):
        rows_nat = slice(t0, t0 + sub_rows)

        def token_major(g, o_ref, l_ref, d):
            rows_view = slice(t0 // d, (t0 + sub_rows) // d)
            if d == 1:
                return o_ref[0, rows_view, :].astype(F32), l_ref[0, rows_view, :]
            slot = dilated_slots.index(g)
            for r in range(d):
                rows = pl.ds(t0 + r, sub_rows // d, stride=d)
                piece = o_ref[0, rows_view, r * ATTN_WIDTH:(r + 1) * ATTN_WIDTH].astype(F32)
                for c in range(n_slabs):
                    o_nat_ref[slot, c, rows, :] = piece[:, c * LANES:(c + 1) * LANES]
                l_nat_ref[slot, rows, :] = l_ref[0, rows_view, r * LANES:(r + 1) * LANES]
            return (jnp.concatenate([o_nat_ref[slot, c, rows_nat, :] for c in range(n_slabs)], axis=1),
                    l_nat_ref[slot, rows_nat, :])

        def conv_tile(ti):
            window = CONV_HALO + sub_rows
            pitch = window // SUBLANES
            u_refs = (u_even_ref, u_odd_ref)
            n_blocks = pitch // 2
            accs = [None] * n_blocks
            for j in range(CONV_WIDTH):
                for b in range(n_blocks):
                    t = (2 * b + j) // 2
                    term = (u_refs[j % 2][ti, t * PACKED_ROWS:(t + 1) * PACKED_ROWS, :].astype(F32)
                            * cw_ref[j * PACKED_ROWS:(j + 1) * PACKED_ROWS, :].astype(F32))
                    accs[b] = term if accs[b] is None else accs[b] + term
            for b in range(n_blocks):
                acc_ref[ti, b * PACKED_ROWS:(b + 1) * PACKED_ROWS, :] = accs[b]
            rows_per_chunk = 4 * SUBLANES
            for r0 in range(0, window, rows_per_chunk):
                y = _layer_norm_rows(acc_ref[ti, r0:r0 + rows_per_chunk, :] + cb_ref[...],
                                     cg_ref[...], cbeta_ref[...])
                y = y * jax.nn.sigmoid(y)
                for i in range(rows_per_chunk // SUBLANES):
                    for c in range(CONV_CHANNELS // LANES):
                        cnat_ref[ti, c, pl.ds(r0 // SUBLANES + i, SUBLANES, stride=pitch), :] = (
                            y[i * SUBLANES:(i + 1) * SUBLANES, c * LANES:(c + 1) * LANES])
            return jnp.concatenate([cnat_ref[ti, c, CONV_HALO:CONV_HALO + sub_rows, :]
                                    for c in range(CONV_CHANNELS // LANES)], axis=1).astype(BF16)

        os_, stats = [], []
        for g, (o_ref, l_ref) in enumerate(((o1_ref, l1_ref), (o2_ref, l2_ref), (o3_ref, l3_ref))):
            o, stat = token_major(g, o_ref, l_ref, DILATED_GROUPS[g][1])
            os_.append(o)
            stats.append(stat)
        lane = lax.broadcasted_iota(jnp.int32, (sub_rows, LANES), 1)
        is_max_lane = (lane % STAT_LANES_PER_HEAD) < STAT_LANES_PER_HEAD // 2
        m = jnp.maximum(jnp.maximum(stats[0], stats[1]), stats[2])
        es = [jnp.exp2(stat - m) for stat in stats]
        dens = [pltpu.roll(stat, LANES - STAT_LANES_PER_HEAD // 2, 1) for stat in stats]
        inv = 1.0 / (es[0] * dens[0] + es[1] * dens[1] + es[2] * dens[2])
        src = lax.broadcasted_iota(jnp.int32, (2 * LANES, ATTN_WIDTH), 0) % LANES
        dst_head = lax.broadcasted_iota(jnp.int32, (2 * LANES, ATTN_WIDTH), 1) // HEAD_DIM
        spread = jnp.where(src == dst_head * STAT_LANES_PER_HEAD, 1.0, 0.0).astype(BF16)
        o_mix = jnp.zeros((sub_rows, ATTN_WIDTH), F32)
        for e, o in zip(es, os_):
            w = jnp.where(is_max_lane, e * inv, 0.0)
            w_hi = w.astype(BF16)
            w_lo = (w - w_hi.astype(F32)).astype(BF16)
            w_full = jnp.dot(jnp.concatenate([w_hi, w_lo], axis=1), spread, preferred_element_type=F32)
            o_mix = o_mix + w_full * o
        y_attn = jnp.dot(o_mix.astype(BF16), wa_ref[...], preferred_element_type=F32)
        y_conv = jnp.dot(conv_tile(t0 // sub_rows), wc_ref[...], preferred_element_type=F32)
        merged = ga_ref[rows_nat, :].astype(F32) * y_attn + gc_ref[rows_nat, :].astype(F32) * y_conv
        mix = jnp.dot(merged.astype(BF16), wo_ref[...], preferred_element_type=F32)
        out_ref[rows_nat, :] = _layer_norm_rows(alpha * x_ref[rows_nat, :] + mix, g_ref[...], b_ref[...])


def _mix_out(os_, stats, u_even, u_odd, conv_taps, conv_b, conv_ln_g, conv_ln_b, ga, gc, x2, wa, wc, wo,
             ln_g, ln_b, alpha, seq, tm, sub_rows):
    tokens = x2.shape[0]
    n_seq_tiles = seq // tm
    n_sub = tm // sub_rows
    window = CONV_HALO + sub_rows
    n_dilated = sum(d > 1 for _, d in DILATED_GROUPS)
    row = lambda i: (i, 0)
    tile3 = lambda i: (i, 0, 0)
    view = lambda i: (i // n_seq_tiles, i % n_seq_tiles, 0)
    const = lambda i: (0, 0)
    resident = lambda a: pl.BlockSpec(a.shape, const, pipeline_mode=pl.Buffered(1))
    o_specs = [pl.BlockSpec((1, tm // d, d * ATTN_WIDTH), view) for _, d in DILATED_GROUPS]
    l_specs = [pl.BlockSpec((1, tm // d, d * LANES), view) for _, d in DILATED_GROUPS]
    return pl.pallas_call(
        functools.partial(_mix_out_kernel, alpha=alpha, sub_rows=sub_rows),
        out_shape=jax.ShapeDtypeStruct((tokens, D_MODEL), F32),
        grid=(tokens // tm,),
        in_specs=(o_specs + l_specs
                  + [pl.BlockSpec((n_sub,) + u_even.shape[1:], tile3),
                     pl.BlockSpec((n_sub,) + u_odd.shape[1:], tile3),
                     resident(conv_taps), resident(conv_b), resident(conv_ln_g), resident(conv_ln_b)]
                  + [pl.BlockSpec((tm, D_MODEL), row)] * 3
                  + [resident(wa), resident(wc), resident(wo), resident(ln_g), resident(ln_b)]),
        out_specs=pl.BlockSpec((tm, D_MODEL), row),
        scratch_shapes=[pltpu.VMEM((n_dilated, ATTN_WIDTH // LANES, tm, LANES), F32),
                        pltpu.VMEM((n_dilated, tm, LANES), F32),
                        pltpu.VMEM((n_sub, window, CONV_CHANNELS), F32),
                        pltpu.VMEM((n_sub, CONV_CHANNELS // LANES, window, LANES), F32)],
        compiler_params=pltpu.CompilerParams(
            dimension_semantics=("parallel",), vmem_limit_bytes=VMEM_LIMIT_BYTES),
        name="mix_out",
    )(*os_, *stats, u_even, u_odd, conv_taps, conv_b, conv_ln_g, conv_ln_b, ga, gc, x2, wa, wc, wo, ln_g, ln_b)


def _ffn_kernel(x_ref, wg_ref, wu_ref, wd_ref, g_ref, b_ref, out_ref, h_ref, *, alpha, chunk, sub_rows):
    d_ff = wg_ref.shape[1]
    for t0 in range(0, x_ref.shape[0], sub_rows):
        rows = slice(t0, t0 + sub_rows)
        x = x_ref[rows, :]
        xb = x.astype(BF16)
        for c0 in range(0, d_ff, chunk):
            gate = jnp.dot(xb, wg_ref[:, c0:c0 + chunk], preferred_element_type=F32)
            up = jnp.dot(xb, wu_ref[:, c0:c0 + chunk], preferred_element_type=F32)
            h_ref[rows, c0:c0 + chunk] = (gate * jax.nn.sigmoid(gate) * up).astype(BF16)
        ff = jnp.dot(h_ref[rows, :], wd_ref[...], preferred_element_type=F32)
        out_ref[rows, :] = _layer_norm_rows(alpha * x + ff, g_ref[...], b_ref[...])


def _ffn(x1, wg, wu, wd, ln_g, ln_b, alpha, tm):
    tokens = x1.shape[0]
    d_ff = wg.shape[1]
    row = lambda i: (i, 0)
    const = lambda i: (0, 0)
    resident = lambda a: pl.BlockSpec(a.shape, const, pipeline_mode=pl.Buffered(1))
    return pl.pallas_call(
        functools.partial(_ffn_kernel, alpha=alpha, chunk=MXU_COLUMNS, sub_rows=tm // 2),
        out_shape=jax.ShapeDtypeStruct((tokens, D_MODEL), F32),
        grid=(tokens // tm,),
        in_specs=[pl.BlockSpec((tm, D_MODEL), row),
                  resident(wg), resident(wu), resident(wd), resident(ln_g), resident(ln_b)],
        out_specs=pl.BlockSpec((tm, D_MODEL), row),
        scratch_shapes=[pltpu.VMEM((tm, d_ff), BF16)],
        compiler_params=pltpu.CompilerParams(
            dimension_semantics=("parallel",), vmem_limit_bytes=VMEM_LIMIT_BYTES),
        name="ffn",
    )(x1, wg, wu, wd, ln_g, ln_b)


def _rotary_tables(seq):
    inv_freq = 1.0 / (ROPE_THETA ** (jnp.arange(0, HEAD_DIM, 2, dtype=F32) / HEAD_DIM))
    hi = (jnp.arange(seq // ROPE_SPLIT, dtype=F32) * ROPE_SPLIT)[:, None] * inv_freq[None, :]
    lo = jnp.arange(ROPE_SPLIT, dtype=F32)[:, None] * inv_freq[None, :]
    cos_hi, sin_hi, cos_lo, sin_lo = lax.optimization_barrier(
        (jnp.cos(hi)[:, None, :], jnp.sin(hi)[:, None, :], jnp.cos(lo)[None], jnp.sin(lo)[None]))
    cos = (cos_hi * cos_lo - sin_hi * sin_lo).reshape(seq, HEAD_DIM // 2)
    sin = (sin_hi * cos_lo + cos_hi * sin_lo).reshape(seq, HEAD_DIM // 2)
    cos_t = jnp.concatenate([cos, cos] * HEADS_PER_VREG, axis=1)
    sin_t = jnp.concatenate([-sin, sin] * HEADS_PER_VREG, axis=1)
    return cos_t, sin_t


def kernel(x, w_in, conv_w, conv_b, conv_ln_g, conv_ln_b, w_attn_out, w_conv_out, w_o, ln1_g, ln1_b,
           w_ffn_gate, w_ffn_up, w_ffn_down, ln2_g, ln2_b):
    batch, seq, d_model = x.shape
    depth = w_in.shape[0]
    alpha = (2 * depth) ** 0.25
    tokens = batch * seq
    assert (d_model == D_MODEL and seq % ATTN_QUERIES_PER_STEP == 0
            and (seq // max(d for _, d in DILATED_GROUPS)) % Q_BLOCK == 0)
    cos_t, sin_t = _rotary_tables(seq)
    vec = lambda a: a.reshape(1, -1)
    h = x.reshape(tokens, d_model)
    for l in range(depth):
        outs = _in_proj(h, w_in[l].astype(BF16), cos_t, sin_t, batch, seq, TOKEN_TILE)
        qkv, (u_even, u_odd, ga, gc) = outs[:3 * N_GROUPS], outs[3 * N_GROUPS:]
        conv_taps = jnp.repeat(conv_w[l].astype(BF16), PACKED_ROWS, axis=0)
        os_, stats = [], []
        for g, (_, dilation) in enumerate(DILATED_GROUPS):
            o, stat = _attention_group(qkv[3 * g], qkv[3 * g + 1], qkv[3 * g + 2], dilation,
                                       queries_per_step=ATTN_QUERIES_PER_STEP)
            os_.append(o)
            stats.append(stat)
        h1 = _mix_out(os_, stats, u_even, u_odd, conv_taps, vec(conv_b[l]), vec(conv_ln_g[l]), vec(conv_ln_b[l]),
                      ga, gc, h, w_attn_out[l].astype(BF16), w_conv_out[l].astype(BF16), w_o[l].astype(BF16),
                      vec(ln1_g[l]), vec(ln1_b[l]), alpha, seq, WIDE_TOKEN_TILE, TOKEN_TILE)
        h = _ffn(h1, w_ffn_gate[l].astype(BF16), w_ffn_up[l].astype(BF16), w_ffn_down[l].astype(BF16),
                 vec(ln2_g[l]), vec(ln2_b[l]), alpha, WIDE_TOKEN_TILE)
    return h.reshape(batch, seq, d_model)
```

```python
import functools

import jax
import jax.numpy as jnp
from jax import lax
from jax.experimental import pallas as pl
from jax.experimental.pallas import tpu as pltpu

D_MODEL = 1024
HEAD_DIM = 64
N_HEADS = 8
ATTN_WIDTH = N_HEADS * HEAD_DIM
DILATED_GROUPS = ((128, 1), (512, 4), (2048, 16))
N_GROUPS = len(DILATED_GROUPS)
CONV_CHANNELS = 512
CONV_WIDTH = 31
ROPE_THETA = 10000.0
LN_EPS = 1e-5
QKV_COLS = N_GROUPS * 3 * ATTN_WIDTH
CONV_COL0 = QKV_COLS
GATE_COL0 = QKV_COLS + 2 * CONV_CHANNELS

LANES = 128
SUBLANES = 8
PACKED_ROWS = 2 * SUBLANES
HEADS_PER_VREG = LANES // HEAD_DIM
N_HEAD_PAIRS = N_HEADS // HEADS_PER_VREG
STAT_LANES_PER_HEAD = LANES // N_HEADS
LOG2_E = 1.4426950408889634
ROPE_SPLIT = 128
Q_BLOCK = 128
CONV_HALO = 32
MASK_VALUE = -1e30
VMEM_LIMIT_BYTES = 56 * 1024 * 1024
MXU_COLUMNS = 256

TOKEN_TILE = 512
WIDE_TOKEN_TILE = 2 * TOKEN_TILE
ATTN_QUERIES_PER_STEP = 4 * TOKEN_TILE

F32 = jnp.float32
BF16 = jnp.bfloat16

assert all(window // dilation == Q_BLOCK for window, dilation in DILATED_GROUPS)


def _layer_norm_rows(h, g, b):
    mu = jnp.mean(h, axis=-1, keepdims=True)
    d = h - mu
    var = jnp.mean(d * d, axis=-1, keepdims=True)
    return d * lax.rsqrt(var + LN_EPS) * g + b


def _in_proj_kernel(x_ref, xh_ref, w_ref, cos_ref, sin_ref, *rest, n_seq_tiles):
    out_refs, (slab_ref, ext_ref) = rest[:-2], rest[-2:]
    qkv_refs = out_refs[:3 * N_GROUPS]
    u_even_ref, u_odd_ref, ga_ref, gc_ref = out_refs[3 * N_GROUPS:]
    tm = x_ref.shape[0]
    n_slabs = x_ref.shape[1] // LANES
    xb = x_ref[...].astype(BF16)
    first_tile = (pl.program_id(0) % n_seq_tiles) == 0
    halo = jnp.where(first_tile, 0.0, xh_ref[...])
    for c in range(n_slabs):
        slab_ref[c, 0:CONV_HALO, :] = halo[:, c * LANES:(c + 1) * LANES]
        slab_ref[c, CONV_HALO:CONV_HALO + tm, :] = x_ref[:, c * LANES:(c + 1) * LANES]
    lane = lax.broadcasted_iota(jnp.int32, (tm, LANES), 1)
    first_half = (lane & (HEAD_DIM // 2)) == 0

    def residue_major(d):
        if d == 1:
            return xb
        rows = [jnp.concatenate([slab_ref[c, pl.ds(CONV_HALO + r, tm // d, stride=d), :]
                                 for c in range(n_slabs)], axis=1) for r in range(d)]
        return jnp.concatenate(rows, axis=0).astype(BF16)

    def table_rows(ref, d):
        if d == 1:
            return ref[...]
        return jnp.concatenate([ref[pl.ds(r, tm // d, stride=d), :] for r in range(d)], axis=0)

    def mm(lhs, col0, width):
        return jnp.dot(lhs, w_ref[:, col0:col0 + width], preferred_element_type=F32)

    window = CONV_HALO + tm
    pitch = window // SUBLANES
    x_win = jnp.concatenate(
        [jnp.concatenate([slab_ref[c, pl.ds(a, SUBLANES, stride=pitch), :] for c in range(n_slabs)], axis=1)
         for a in range(pitch)], axis=0).astype(BF16)
    u = mm(x_win, CONV_COL0, CONV_CHANNELS) * jax.nn.sigmoid(mm(x_win, CONV_COL0 + CONV_CHANNELS, CONV_CHANNELS))
    n_wrap = CONV_WIDTH - 1
    ext_ref[n_wrap * SUBLANES:, :] = u
    ext_ref[0:n_wrap * SUBLANES, :] = ext_ref[pitch * SUBLANES - 1:(pitch + n_wrap) * SUBLANES - 1, :]
    u_even_ref[0] = ext_ref[...].astype(BF16)
    u_odd_ref[0] = ext_ref[SUBLANES:SUBLANES + u_odd_ref.shape[1], :].astype(BF16)

    def rope(h, cos_t, sin_t):
        outs = []
        for c in range(h.shape[1] // LANES):
            hs = h[:, c * LANES:(c + 1) * LANES]
            partner = jnp.where(first_half,
                                pltpu.roll(hs, LANES - HEAD_DIM // 2, 1),
                                pltpu.roll(hs, HEAD_DIM // 2, 1))
            outs.append(hs * cos_t + partner * sin_t)
        return jnp.concatenate(outs, axis=1)

    def store_residue_view(ref, h, d):
        rows = tm // d
        for r in range(d):
            ref[0, :, r * ATTN_WIDTH:(r + 1) * ATTN_WIDTH] = h[r * rows:(r + 1) * rows, :].astype(BF16)

    for g, (_, d) in enumerate(DILATED_GROUPS):
        base = g * 3 * ATTN_WIDTH
        xg = residue_major(d)
        cos_t = table_rows(cos_ref, d)
        sin_t = table_rows(sin_ref, d)
        q = rope(mm(xg, base, ATTN_WIDTH), cos_t, sin_t) * (LOG2_E * HEAD_DIM ** -0.5)
        store_residue_view(qkv_refs[3 * g], q, d)
        store_residue_view(qkv_refs[3 * g + 1], rope(mm(xg, base + ATTN_WIDTH, ATTN_WIDTH), cos_t, sin_t), d)
        store_residue_view(qkv_refs[3 * g + 2], mm(xg, base + 2 * ATTN_WIDTH, ATTN_WIDTH), d)

    ga_ref[...] = jax.nn.sigmoid(mm(xb, GATE_COL0, D_MODEL)).astype(BF16)
    gc_ref[...] = jax.nn.sigmoid(mm(xb, GATE_COL0 + D_MODEL, D_MODEL)).astype(BF16)


def _conv_window_rows(tm):
    return (CONV_WIDTH - 1) * SUBLANES + CONV_HALO + tm


def _in_proj(x2, w_in_b, cos_t, sin_t, batch, seq, tm):
    tokens = x2.shape[0]
    n_tiles = tokens // tm
    n_seq_tiles = seq // tm
    halo_blocks_per_tile = tm // CONV_HALO
    window = CONV_HALO + tm
    assert window % SUBLANES == 0 and (window // SUBLANES) % 8 != 0
    ext_rows = _conv_window_rows(tm)
    row = lambda i: (i, 0)
    const = lambda i: (0, 0)
    tab = lambda i: (i % n_seq_tiles, 0)
    tile3 = lambda i: (i, 0, 0)
    view = lambda i: (i // n_seq_tiles, i % n_seq_tiles, 0)
    qkv_shapes, qkv_specs = [], []
    for _, d in DILATED_GROUPS:
        qkv_shapes += [jax.ShapeDtypeStruct((batch, seq // d, d * ATTN_WIDTH), BF16)] * 3
        qkv_specs += [pl.BlockSpec((1, tm // d, d * ATTN_WIDTH), view)] * 3
    u_rows = (ext_rows, ext_rows - PACKED_ROWS)
    out_shape = (qkv_shapes
                 + [jax.ShapeDtypeStruct((n_tiles, r, CONV_CHANNELS), BF16) for r in u_rows]
                 + [jax.ShapeDtypeStruct((tokens, D_MODEL), BF16)] * 2)
    out_specs = (qkv_specs
                 + [pl.BlockSpec((1, r, CONV_CHANNELS), tile3) for r in u_rows]
                 + [pl.BlockSpec((tm, D_MODEL), row)] * 2)
    return pl.pallas_call(
        functools.partial(_in_proj_kernel, n_seq_tiles=n_seq_tiles),
        out_shape=out_shape,
        grid=(n_tiles,),
        in_specs=[
            pl.BlockSpec((tm, D_MODEL), row),
            pl.BlockSpec((CONV_HALO, D_MODEL), lambda i: (jnp.maximum(i * halo_blocks_per_tile - 1, 0), 0)),
            pl.BlockSpec(w_in_b.shape, const, pipeline_mode=pl.Buffered(1)),
            pl.BlockSpec((tm, LANES), tab),
            pl.BlockSpec((tm, LANES), tab),
        ],
        out_specs=out_specs,
        scratch_shapes=[pltpu.VMEM((D_MODEL // LANES, window, LANES), F32),
                        pltpu.VMEM((ext_rows, CONV_CHANNELS), F32)],
        compiler_params=pltpu.CompilerParams(
            dimension_semantics=("parallel",), vmem_limit_bytes=VMEM_LIMIT_BYTES),
        name="in_proj",
    )(x2, x2, w_in_b, cos_t, sin_t)


def _conv_kernel(u_even_ref, u_odd_ref, cw_ref, cb_ref, cg_ref, cbeta_ref, c_ref, acc_ref, cnat_ref):
    tm = c_ref.shape[0]
    window = CONV_HALO + tm
    pitch = window // SUBLANES
    u_refs = (u_even_ref, u_odd_ref)
    n_blocks = pitch // 2
    accs = [None] * n_blocks
    for j in range(CONV_WIDTH):
        for b in range(n_blocks):
            t = (2 * b + j) // 2
            term = (u_refs[j % 2][0, t * PACKED_ROWS:(t + 1) * PACKED_ROWS, :].astype(F32)
                    * cw_ref[j * PACKED_ROWS:(j + 1) * PACKED_ROWS, :].astype(F32))
            accs[b] = term if accs[b] is None else accs[b] + term
    for b in range(n_blocks):
        acc_ref[b * PACKED_ROWS:(b + 1) * PACKED_ROWS, :] = accs[b]
    rows_per_chunk = 4 * SUBLANES
    for r0 in range(0, window, rows_per_chunk):
        y = _layer_norm_rows(acc_ref[r0:r0 + rows_per_chunk, :] + cb_ref[...], cg_ref[...], cbeta_ref[...])
        y = y * jax.nn.sigmoid(y)
        for i in range(rows_per_chunk // SUBLANES):
            for c in range(CONV_CHANNELS // LANES):
                cnat_ref[c, pl.ds(r0 // SUBLANES + i, SUBLANES, stride=pitch), :] = (
                    y[i * SUBLANES:(i + 1) * SUBLANES, c * LANES:(c + 1) * LANES])
    c_ref[...] = jnp.concatenate([cnat_ref[c, CONV_HALO:CONV_HALO + tm, :]
                                  for c in range(CONV_CHANNELS // LANES)], axis=1).astype(BF16)


def _conv_branch(u_even, u_odd, conv_taps, conv_b, ln_g, ln_b, tm):
    n_tiles = u_even.shape[0]
    window = CONV_HALO + tm
    const = lambda i: (0, 0)
    tile3 = lambda i: (i, 0, 0)
    vec = pl.BlockSpec((1, CONV_CHANNELS), const)
    return pl.pallas_call(
        _conv_kernel,
        out_shape=jax.ShapeDtypeStruct((n_tiles * tm, CONV_CHANNELS), BF16),
        grid=(n_tiles,),
        in_specs=[pl.BlockSpec((1,) + u_even.shape[1:], tile3),
                  pl.BlockSpec((1,) + u_odd.shape[1:], tile3),
                  pl.BlockSpec(conv_taps.shape, const),
                  vec, vec, vec],
        out_specs=pl.BlockSpec((tm, CONV_CHANNELS), lambda i: (i, 0)),
        scratch_shapes=[pltpu.VMEM((window, CONV_CHANNELS), F32),
                        pltpu.VMEM((CONV_CHANNELS // LANES, window, LANES), F32)],
        compiler_params=pltpu.CompilerParams(
            dimension_semantics=("parallel",), vmem_limit_bytes=VMEM_LIMIT_BYTES),
        name="conv_branch",
    )(u_even, u_odd, conv_taps, conv_b, ln_g, ln_b)


def _attn_kernel(q_ref, kc_ref, kp_ref, vc_ref, vp_ref, o_ref, stat_ref, bias_ref):
    tq = q_ref.shape[1]
    tile = pl.program_id(2)

    def band(prev_ref, cur_ref, j, cols):
        if j == 0:
            return jnp.concatenate([prev_ref[0, :, cols], cur_ref[0, 0:Q_BLOCK, cols]], axis=0)
        return cur_ref[0, (j - 1) * Q_BLOCK:(j + 1) * Q_BLOCK, cols]

    rows2 = HEADS_PER_VREG * Q_BLOCK
    keys = 2 * Q_BLOCK
    qi = lax.broadcasted_iota(jnp.int32, (rows2, keys), 0) & (Q_BLOCK - 1)
    kk = lax.broadcasted_iota(jnp.int32, (rows2, keys), 1)
    dist = qi + Q_BLOCK - kk
    in_window = (dist >= 0) & (dist <= Q_BLOCK)
    bias_ref[0] = jnp.where(in_window, 0.0, MASK_VALUE)
    bias_ref[1] = jnp.where(in_window & (kk >= Q_BLOCK), 0.0, MASK_VALUE)

    lane = lax.broadcasted_iota(jnp.int32, (Q_BLOCK, LANES), 1)
    head_a = lane < HEAD_DIM
    stat_head = lane // STAT_LANES_PER_HEAD
    stat_is_max = (lane % STAT_LANES_PER_HEAD) < STAT_LANES_PER_HEAD // 2
    ones_cols = jnp.ones((keys, LANES), BF16)

    n_residues = q_ref.shape[2] // ATTN_WIDTH
    for res, j in [(res, j) for res in range(n_residues) for j in range(tq // Q_BLOCK)]:
        r0 = j * Q_BLOCK
        bias = bias_ref[(tile == 0).astype(jnp.int32)] if j == 0 else bias_ref[0]
        stat_tile = jnp.zeros((Q_BLOCK, LANES), F32)
        for hp in range(N_HEAD_PAIRS):
            cols = slice(res * ATTN_WIDTH + hp * LANES, res * ATTN_WIDTH + (hp + 1) * LANES)
            q2 = q_ref[0, r0:r0 + Q_BLOCK, cols]
            zero = jnp.zeros_like(q2)
            qs = jnp.concatenate([jnp.where(head_a, q2, zero), jnp.where(head_a, zero, q2)], axis=0)
            k2 = band(kp_ref, kc_ref, j, cols)
            v2 = band(vp_ref, vc_ref, j, cols)
            s = lax.dot_general(qs, k2, (((1,), (1,)), ((), ())), preferred_element_type=F32) + bias
            m = jnp.max(s, axis=1, keepdims=True)
            p = jnp.exp2(s - m).astype(BF16)
            ov = jnp.dot(p, jnp.concatenate([v2, ones_cols], axis=1), preferred_element_type=F32)
            o_ref[0, r0:r0 + Q_BLOCK, cols] = jnp.where(
                head_a, ov[:Q_BLOCK, :LANES], ov[Q_BLOCK:, :LANES]).astype(BF16)
            m_lanes = jnp.broadcast_to(m, (rows2, LANES))
            for half in range(HEADS_PER_VREG):
                rows = slice(half * Q_BLOCK, (half + 1) * Q_BLOCK)
                stat = jnp.where(stat_is_max, m_lanes[rows], ov[rows, LANES:])
                stat_tile = jnp.where(stat_head == HEADS_PER_VREG * hp + half, stat, stat_tile)
        stat_ref[0, r0:r0 + Q_BLOCK, res * LANES:(res + 1) * LANES] = stat_tile


def _attention_group(q3, k3, v3, dilation, queries_per_step):
    batch, sub_len, _ = q3.shape
    tq = min(queries_per_step, sub_len)
    n_res = queries_per_step // tq
    assert dilation % n_res == 0 and sub_len % tq == 0
    prev_blocks_per_tile = tq // Q_BLOCK
    cur = lambda b, r, n: (b, n, r)
    prev = lambda b, r, n: (b, jnp.maximum(n * prev_blocks_per_tile - 1, 0), r)
    return pl.pallas_call(
        _attn_kernel,
        out_shape=[jax.ShapeDtypeStruct(q3.shape, BF16),
                   jax.ShapeDtypeStruct((batch, sub_len, dilation * LANES), F32)],
        grid=(batch, dilation // n_res, sub_len // tq),
        in_specs=[
            pl.BlockSpec((1, tq, n_res * ATTN_WIDTH), cur),
            pl.BlockSpec((1, tq, n_res * ATTN_WIDTH), cur),
            pl.BlockSpec((1, Q_BLOCK, n_res * ATTN_WIDTH), prev),
            pl.BlockSpec((1, tq, n_res * ATTN_WIDTH), cur),
            pl.BlockSpec((1, Q_BLOCK, n_res * ATTN_WIDTH), prev),
        ],
        out_specs=[pl.BlockSpec((1, tq, n_res * ATTN_WIDTH), cur),
                   pl.BlockSpec((1, tq, n_res * LANES), cur)],
        scratch_shapes=[pltpu.VMEM((2, HEADS_PER_VREG * Q_BLOCK, 2 * Q_BLOCK), F32)],
        compiler_params=pltpu.CompilerParams(
            dimension_semantics=("parallel", "parallel", "arbitrary"),
            vmem_limit_bytes=VMEM_LIMIT_BYTES),
        name=f"attn_d{dilation}",
    )(q3, k3, k3, v3, v3)


def _mix_out_kernel(o1_ref, o2_ref, o3_ref, l1_ref, l2_ref, l3_ref, c_ref, ga_ref, gc_ref, x_ref,
                    wa_ref, wc_ref, wo_ref, g_ref, b_ref, out_ref, o_nat_ref, l_nat_ref, *, alpha, sub_rows):
    tm = x_ref.shape[0]
    n_slabs = ATTN_WIDTH // LANES
    for t0 in range(0, tm, sub_rows):
        rows_nat = slice(t0, t0 + sub_rows)

        def token_major(g, o_ref, l_ref, d):
            rows_view = slice(t0 // d, (t0 + sub_rows) // d)
            if d == 1:
                return o_ref[0, rows_view, :].astype(F32), l_ref[0, rows_view, :]
            for r in range(d):
                rows = pl.ds(t0 + r, sub_rows // d, stride=d)
                piece = o_ref[0, rows_view, r * ATTN_WIDTH:(r + 1) * ATTN_WIDTH].astype(F32)
                for c in range(n_slabs):
                    o_nat_ref[g, c, rows, :] = piece[:, c * LANES:(c + 1) * LANES]
                l_nat_ref[g, rows, :] = l_ref[0, rows_view, r * LANES:(r + 1) * LANES]
            return (jnp.concatenate([o_nat_ref[g, c, rows_nat, :] for c in range(n_slabs)], axis=1),
                    l_nat_ref[g, rows_nat, :])

        os_, stats = [], []
        for g, (o_ref, l_ref) in enumerate(((o1_ref, l1_ref), (o2_ref, l2_ref), (o3_ref, l3_ref))):
            o, stat = token_major(g, o_ref, l_ref, DILATED_GROUPS[g][1])
            os_.append(o)
            stats.append(stat)
        lane = lax.broadcasted_iota(jnp.int32, (sub_rows, LANES), 1)
        is_max_lane = (lane % STAT_LANES_PER_HEAD) < STAT_LANES_PER_HEAD // 2
        m = jnp.maximum(jnp.maximum(stats[0], stats[1]), stats[2])
        es = [jnp.exp2(stat - m) for stat in stats]
        dens = [pltpu.roll(stat, LANES - STAT_LANES_PER_HEAD // 2, 1) for stat in stats]
        inv = 1.0 / (es[0] * dens[0] + es[1] * dens[1] + es[2] * dens[2])
        src = lax.broadcasted_iota(jnp.int32, (2 * LANES, ATTN_WIDTH), 0) % LANES
        dst_head = lax.broadcasted_iota(jnp.int32, (2 * LANES, ATTN_WIDTH), 1) // HEAD_DIM
        spread = jnp.where(src == dst_head * STAT_LANES_PER_HEAD, 1.0, 0.0).astype(BF16)
        o_mix = jnp.zeros((sub_rows, ATTN_WIDTH), F32)
        for e, o in zip(es, os_):
            w = jnp.where(is_max_lane, e * inv, 0.0)
            w_hi = w.astype(BF16)
            w_lo = (w - w_hi.astype(F32)).astype(BF16)
            w_full = jnp.dot(jnp.concatenate([w_hi, w_lo], axis=1), spread, preferred_element_type=F32)
            o_mix = o_mix + w_full * o
        y_attn = jnp.dot(o_mix.astype(BF16), wa_ref[...], preferred_element_type=F32)
        y_conv = jnp.dot(c_ref[rows_nat, :], wc_ref[...], preferred_element_type=F32)
        merged = ga_ref[rows_nat, :].astype(F32) * y_attn + gc_ref[rows_nat, :].astype(F32) * y_conv
        mix = jnp.dot(merged.astype(BF16), wo_ref[...], preferred_element_type=F32)
        out_ref[rows_nat, :] = _layer_norm_rows(alpha * x_ref[rows_nat, :] + mix, g_ref[...], b_ref[...])


def _mix_out(os_, stats, c, ga, gc, x2, wa, wc, wo, ln_g, ln_b, alpha, seq, tm):
    tokens = x2.shape[0]
    n_seq_tiles = seq // tm
    row = lambda i: (i, 0)
    view = lambda i: (i // n_seq_tiles, i % n_seq_tiles, 0)
    const = lambda i: (0, 0)
    resident = lambda a: pl.BlockSpec(a.shape, const, pipeline_mode=pl.Buffered(1))
    o_specs = [pl.BlockSpec((1, tm // d, d * ATTN_WIDTH), view) for _, d in DILATED_GROUPS]
    l_specs = [pl.BlockSpec((1, tm // d, d * LANES), view) for _, d in DILATED_GROUPS]
    return pl.pallas_call(
        functools.partial(_mix_out_kernel, alpha=alpha, sub_rows=tm // 2),
        out_shape=jax.ShapeDtypeStruct((tokens, D_MODEL), F32),
        grid=(tokens // tm,),
        in_specs=(o_specs + l_specs
                  + [pl.BlockSpec((tm, CONV_CHANNELS), row)]
                  + [pl.BlockSpec((tm, D_MODEL), row)] * 3
                  + [resident(wa), resident(wc), resident(wo), resident(ln_g), resident(ln_b)]),
        out_specs=pl.BlockSpec((tm, D_MODEL), row),
        scratch_shapes=[pltpu.VMEM((N_GROUPS, ATTN_WIDTH // LANES, tm, LANES), F32),
                        pltpu.VMEM((N_GROUPS, tm, LANES), F32)],
        compiler_params=pltpu.CompilerParams(
            dimension_semantics=("parallel",), vmem_limit_bytes=VMEM_LIMIT_BYTES),
        name="mix_out",
    )(*os_, *stats, c, ga, gc, x2, wa, wc, wo, ln_g, ln_b)


def _ffn_kernel(x_ref, wg_ref, wu_ref, wd_ref, g_ref, b_ref, out_ref, h_ref, *, alpha, chunk, k_group,
                sub_rows):
    d_ff = wg_ref.shape[1]
    for t0 in range(0, x_ref.shape[0], sub_rows):
        rows = slice(t0, t0 + sub_rows)
        x = x_ref[rows, :]
        xb = x.astype(BF16)
        ff = None
        for k0 in range(0, d_ff, k_group):
            k1 = min(k0 + k_group, d_ff)
            for c0 in range(k0, k1, chunk):
                gate = jnp.dot(xb, wg_ref[:, c0:c0 + chunk], preferred_element_type=F32)
                up = jnp.dot(xb, wu_ref[:, c0:c0 + chunk], preferred_element_type=F32)
                h_ref[rows, c0:c0 + chunk] = (gate * jax.nn.sigmoid(gate) * up).astype(BF16)
            part = jnp.dot(h_ref[rows, k0:k1], wd_ref[k0:k1, :], preferred_element_type=F32)
            ff = part if ff is None else ff + part
        out_ref[rows, :] = _layer_norm_rows(alpha * x + ff, g_ref[...], b_ref[...])


def _ffn(x1, wg, wu, wd, ln_g, ln_b, alpha, tm):
    tokens = x1.shape[0]
    d_ff = wg.shape[1]
    row = lambda i: (i, 0)
    const = lambda i: (0, 0)
    resident = lambda a: pl.BlockSpec(a.shape, const, pipeline_mode=pl.Buffered(1))
    return pl.pallas_call(
        functools.partial(_ffn_kernel, alpha=alpha, chunk=MXU_COLUMNS, k_group=4 * MXU_COLUMNS,
                          sub_rows=tm // 2),
        out_shape=jax.ShapeDtypeStruct((tokens, D_MODEL), F32),
        grid=(tokens // tm,),
        in_specs=[pl.BlockSpec((tm, D_MODEL), row),
                  resident(wg), resident(wu), resident(wd), resident(ln_g), resident(ln_b)],
        out_specs=pl.BlockSpec((tm, D_MODEL), row),
        scratch_shapes=[pltpu.VMEM((tm, d_ff), BF16)],
        compiler_params=pltpu.CompilerParams(
            dimension_semantics=("parallel",), vmem_limit_bytes=VMEM_LIMIT_BYTES),
        name="ffn",
    )(x1, wg, wu, wd, ln_g, ln_b)


def _rotary_tables(seq):
    inv_freq = 1.0 / (ROPE_THETA ** (jnp.arange(0, HEAD_DIM, 2, dtype=F32) / HEAD_DIM))
    hi = (jnp.arange(seq // ROPE_SPLIT, dtype=F32) * ROPE_SPLIT)[:, None] * inv_freq[None, :]
    lo = jnp.arange(ROPE_SPLIT, dtype=F32)[:, None] * inv_freq[None, :]
    cos_hi, sin_hi, cos_lo, sin_lo = lax.optimization_barrier(
        (jnp.cos(hi)[:, None, :], jnp.sin(hi)[:, None, :], jnp.cos(lo)[None], jnp.sin(lo)[None]))
    cos = (cos_hi * cos_lo - sin_hi * sin_lo).reshape(seq, HEAD_DIM // 2)
    sin = (sin_hi * cos_lo + cos_hi * sin_lo).reshape(seq, HEAD_DIM // 2)
    cos_t = jnp.concatenate([cos, cos] * HEADS_PER_VREG, axis=1)
    sin_t = jnp.concatenate([-sin, sin] * HEADS_PER_VREG, axis=1)
    return cos_t, sin_t


def kernel(x, w_in, conv_w, conv_b, conv_ln_g, conv_ln_b, w_attn_out, w_conv_out, w_o, ln1_g, ln1_b,
           w_ffn_gate, w_ffn_up, w_ffn_down, ln2_g, ln2_b):
    batch, seq, d_model = x.shape
    depth = w_in.shape[0]
    alpha = (2 * depth) ** 0.25
    tokens = batch * seq
    assert (d_model == D_MODEL and seq % ATTN_QUERIES_PER_STEP == 0
            and (seq // max(d for _, d in DILATED_GROUPS)) % Q_BLOCK == 0)
    cos_t, sin_t = _rotary_tables(seq)
    vec = lambda a: a.reshape(1, -1)
    h = x.reshape(tokens, d_model)
    for l in range(depth):
        outs = _in_proj(h, w_in[l].astype(BF16), cos_t, sin_t, batch, seq, TOKEN_TILE)
        qkv, (u_even, u_odd, ga, gc) = outs[:3 * N_GROUPS], outs[3 * N_GROUPS:]
        conv_taps = jnp.repeat(conv_w[l].astype(BF16), PACKED_ROWS, axis=0)
        c = _conv_branch(u_even, u_odd, conv_taps, vec(conv_b[l]), vec(conv_ln_g[l]), vec(conv_ln_b[l]),
                         TOKEN_TILE)
        os_, stats = [], []
        for g, (_, dilation) in enumerate(DILATED_GROUPS):
            o, stat = _attention_group(qkv[3 * g], qkv[3 * g + 1], qkv[3 * g + 2], dilation,
                                       queries_per_step=ATTN_QUERIES_PER_STEP)
            os_.append(o)
            stats.append(stat)
        h1 = _mix_out(os_, stats, c, ga, gc, h, w_attn_out[l].astype(BF16), w_conv_out[l].astype(BF16),
                      w_o[l].astype(BF16), vec(ln1_g[l]), vec(ln1_b[l]), alpha, seq, WIDE_TOKEN_TILE)
        h = _ffn(h1, w_ffn_gate[l].astype(BF16), w_ffn_up[l].astype(BF16), w_ffn_down[l].astype(BF16),
                 vec(ln2_g[l]), vec(ln2_b[l]), alpha, WIDE_TOKEN_TILE)
    return h.reshape(batch, seq, d_model)
```
